```python
import math
import jax, jax.numpy as jnp
from jax import lax
import numpy as np

D_MODEL = 1024
BATCH = 2
SEQ = 8192
DEPTH = 1

CONV_CH = D_MODEL
CONV_K = 3
N_HEADS = 8
HEAD_DIM = D_MODEL // (2 * N_HEADS)
V_DIM = 2 * HEAD_DIM
QK_WIDTH = N_HEADS * 2 * HEAD_DIM
ATTN_WIDTH = N_HEADS * V_DIM
ROPE_THETA = 10000.0
Q_BLOCK = 128
SPLIT_SIZES = (CONV_CH, CONV_CH, CONV_CH, QK_WIDTH, QK_WIDTH, ATTN_WIDTH, D_MODEL, D_MODEL)
IN_WIDTH = sum(SPLIT_SIZES)
SPLIT_POINTS = tuple(int(p) for p in np.cumsum(SPLIT_SIZES)[:-1])
PEER_HEADS = 8
PEER_QDIM = 256
PEER_HALF = PEER_QDIM // 2
N_KEYS = 128
N_EXPERTS = N_KEYS * N_KEYS
PEER_TOPK = 16
TOK_BLOCK = 128
RMS_EPS = 1e-6
NEG_INF = -1e30

kernel_name = 'hybrid_shortconv_diffattn_peer'


def rms_norm(x, g):
    xf = x.astype(jnp.float32)
    y = xf * lax.rsqrt(jnp.mean(xf * xf, axis=-1, keepdims=True) + RMS_EPS)
    return (y * g.astype(jnp.float32)).astype(x.dtype)


def rotary(x, pos):
    d = x.shape[-1]
    inv = ROPE_THETA ** (-jnp.arange(0, d, 2, dtype=jnp.float32) / d)
    ang = pos.astype(jnp.float32)[:, None] * inv[None, :]
    cos = jnp.cos(ang)[None, :, None, :]
    sin = jnp.sin(ang)[None, :, None, :]
    xf = x.astype(jnp.float32)
    x1, x2 = xf[..., : d // 2], xf[..., d // 2:]
    return jnp.concatenate([x1 * cos - x2 * sin, x2 * cos + x1 * sin], axis=-1).astype(x.dtype)


def short_conv_mixer(b, c, u, conv_w, w_out):
    z_in = c * u
    z = lax.conv_general_dilated(
        z_in, conv_w[:, None, :].astype(z_in.dtype), window_strides=(1,),
        padding=[(CONV_K - 1, 0)], dimension_numbers=('NWC', 'WIO', 'NWC'),
        feature_group_count=z_in.shape[-1])
    return (b * z) @ w_out


def diff_attention(q, k, v, lam, subln_g, lam_init):
    B, S = q.shape[0], q.shape[1]
    qh = jnp.transpose(q, (0, 2, 3, 1, 4))
    kh = jnp.transpose(k, (0, 2, 3, 1, 4))
    vh = jnp.transpose(v, (0, 2, 1, 3))
    scale = HEAD_DIM ** -0.5
    kpos = jnp.arange(S)

    def block(i):
        qb = lax.dynamic_slice_in_dim(qh, i * Q_BLOCK, Q_BLOCK, axis=3)
        s = jnp.einsum('bhmqd,bhmkd->bhmqk', qb, kh).astype(jnp.float32) * scale
        qpos = i * Q_BLOCK + jnp.arange(Q_BLOCK)
        mask = kpos[None, :] <= qpos[:, None]
        p = jax.nn.softmax(jnp.where(mask, s, NEG_INF), axis=-1)
        a = p[:, :, 0] - lam * p[:, :, 1]
        return jnp.einsum('bhqk,bhkd->bhqd', a.astype(vh.dtype), vh)

    o = lax.map(block, jnp.arange(S // Q_BLOCK))
    o = jnp.transpose(o, (1, 0, 3, 2, 4)).reshape(B, S, N_HEADS, V_DIM)
    o = rms_norm(o, subln_g) * (1.0 - lam_init)
    return o.reshape(B, S, ATTN_WIDTH)


def peer_ffn(xn, w_q, keys1, keys2, u_tab, v_tab):
    B, S, D = xn.shape
    q = (xn @ w_q).reshape(B, S, PEER_HEADS, 2, PEER_HALF)
    s1 = jnp.einsum('bshd,kd->bshk', q[..., 0, :], keys1).astype(jnp.float32)
    s2 = jnp.einsum('bshd,kd->bshk', q[..., 1, :], keys2).astype(jnp.float32)
    t1, i1 = lax.top_k(s1, PEER_TOPK)
    t2, i2 = lax.top_k(s2, PEER_TOPK)
    cand = (t1[..., :, None] + t2[..., None, :]).reshape(B, S, PEER_HEADS, PEER_TOPK * PEER_TOPK)
    top, ix = lax.top_k(cand, PEER_TOPK)
    e1 = jnp.take_along_axis(i1, ix // PEER_TOPK, axis=-1)
    e2 = jnp.take_along_axis(i2, ix % PEER_TOPK, axis=-1)
    experts = e1 * N_KEYS + e2
    gates = jax.nn.softmax(top, axis=-1).astype(xn.dtype)
    nb = S // TOK_BLOCK

    def to_blocks(a):
        return jnp.swapaxes(a.reshape((B, nb, TOK_BLOCK) + a.shape[2:]), 0, 1)

    def block(args):
        xb, eb, gb = args
        u = u_tab[eb]
        v = v_tab[eb]
        act = jax.nn.gelu(jnp.einsum('btd,bthkd->bthk', xb, u), approximate=False) * gb
        return jnp.einsum('bthk,bthkd->btd', act, v)

    y = lax.map(block, (to_blocks(xn), to_blocks(experts), to_blocks(gates)))
    return jnp.swapaxes(y, 0, 1).reshape(B, S, D)


def setup_inputs(seed: int = 0) -> dict:
    key = jax.random.key(seed)
    ks = jax.random.split(key, 20)
    f32 = jnp.float32
    nrm = lambda k, shape, s: jax.random.normal(k, shape, f32) * s
    return {
        'x': nrm(ks[0], (BATCH, SEQ, D_MODEL), 1.0),
        'norm1_g': 1.0 + nrm(ks[1], (DEPTH, D_MODEL), 0.02),
        'w_in': nrm(ks[2], (DEPTH, D_MODEL, IN_WIDTH), D_MODEL ** -0.5),
        'conv_w': nrm(ks[3], (DEPTH, CONV_K, CONV_CH), CONV_K ** -0.5),
        'w_out_conv': nrm(ks[4], (DEPTH, CONV_CH, D_MODEL), CONV_CH ** -0.5),
        'lambda_q1': nrm(ks[5], (DEPTH, HEAD_DIM), 0.1),
        'lambda_k1': nrm(ks[6], (DEPTH, HEAD_DIM), 0.1),
        'lambda_q2': nrm(ks[7], (DEPTH, HEAD_DIM), 0.1),
        'lambda_k2': nrm(ks[8], (DEPTH, HEAD_DIM), 0.1),
        'attn_subln_g': 1.0 + nrm(ks[9], (DEPTH, V_DIM), 0.02),
        'w_out_attn': nrm(ks[10], (DEPTH, ATTN_WIDTH, D_MODEL), ATTN_WIDTH ** -0.5),
        'w_o': nrm(ks[11], (DEPTH, D_MODEL, D_MODEL), D_MODEL ** -0.5),
        'norm2_g': 1.0 + nrm(ks[12], (DEPTH, D_MODEL), 0.02),
        'peer_w_q': nrm(ks[13], (DEPTH, D_MODEL, PEER_HEADS * PEER_QDIM), D_MODEL ** -0.5),
        'peer_keys1': nrm(ks[14], (DEPTH, N_KEYS, PEER_HALF), PEER_HALF ** -0.5),
        'peer_keys2': nrm(ks[15], (DEPTH, N_KEYS, PEER_HALF), PEER_HALF ** -0.5),
        'peer_u': nrm(ks[16], (DEPTH, N_EXPERTS, D_MODEL), D_MODEL ** -0.5),
        'peer_v': nrm(ks[17], (DEPTH, N_EXPERTS, D_MODEL), 0.1),
        'final_norm_g': 1.0 + nrm(ks[18], (D_MODEL,), 0.02),
    }


def reference(x, norm1_g, w_in, conv_w, w_out_conv, lambda_q1, lambda_k1, lambda_q2, lambda_k2,
              attn_subln_g, w_out_attn, w_o, norm2_g, peer_w_q, peer_keys1, peer_keys2,
              peer_u, peer_v, final_norm_g):
    B, S, _ = x.shape
    pos = jnp.arange(S, dtype=jnp.int32)
    h = x
    for l in range(DEPTH):
        lam_init = 0.8 - 0.6 * math.exp(-0.3 * l)
        xn = rms_norm(h, norm1_g[l])
        proj = xn @ w_in[l]
        cb, cc, cu, q, k, v, g_conv, g_attn = jnp.split(proj, SPLIT_POINTS, axis=-1)
        y_conv = short_conv_mixer(cb, cc, cu, conv_w[l], w_out_conv[l])
        q = rotary(q.reshape(B, S, 2 * N_HEADS, HEAD_DIM), pos).reshape(B, S, N_HEADS, 2, HEAD_DIM)
        k = rotary(k.reshape(B, S, 2 * N_HEADS, HEAD_DIM), pos).reshape(B, S, N_HEADS, 2, HEAD_DIM)
        lam = (jnp.exp(jnp.sum(lambda_q1[l].astype(jnp.float32) * lambda_k1[l].astype(jnp.float32)))
               - jnp.exp(jnp.sum(lambda_q2[l].astype(jnp.float32) * lambda_k2[l].astype(jnp.float32)))
               + lam_init)
        y_attn = diff_attention(q, k, v.reshape(B, S, N_HEADS, V_DIM), lam,
                                attn_subln_g[l], lam_init) @ w_out_attn[l]
        mixed = jax.nn.sigmoid(g_conv) * y_conv + jax.nn.sigmoid(g_attn) * y_attn
        h = h + mixed @ w_o[l]
        h = h + peer_ffn(rms_norm(h, norm2_g[l]), peer_w_q[l], peer_keys1[l], peer_keys2[l],
                         peer_u[l], peer_v[l])
    return rms_norm(h, final_norm_g)
```

```python
import functools
import math

import jax
import jax.numpy as jnp
from jax import lax
from jax.experimental import pallas as pl
from jax.experimental.pallas import tpu as pltpu

F32 = jnp.float32
BF16 = jnp.bfloat16

D_MODEL = 1024
N_HEADS = 8
HEAD_DIM = 64
V_DIM = 2 * HEAD_DIM
CONV_K = 3
ROPE_THETA = 10000.0
RMS_EPS = 1e-6
LAM_INIT = 0.8 - 0.6 * math.exp(-0.3 * 0)
NEG_INF = -1e30

PEER_HEADS = 8
PEER_HALF = 128
N_KEYS = 128
PEER_TOPK = 16

LANES = 128
BF16_SUBLANES = 16

COL_B, COL_C, COL_U, COL_Q, COL_K, COL_V, COL_GC, COL_GA = range(8)

VMEM_LIMIT = 56 * 1024 * 1024


def _rms(x, g):
    ms = jnp.mean(x * x, axis=-1, keepdims=True)
    return x * lax.rsqrt(ms + RMS_EPS) * g


def _inproj_kernel(x_ref, g_ref, w_ref, cos_ref, sin_ref, o_ref, xn_ref):
    j = pl.program_id(1)

    @pl.when(j == 0)
    def _():
        xn_ref[...] = _rms(x_ref[...], g_ref[...]).astype(BF16)

    acc = jnp.dot(xn_ref[...], w_ref[...], preferred_element_type=F32)
    is_rot = jnp.logical_or(j == COL_Q, j == COL_K)

    @pl.when(jnp.logical_not(is_rot))
    def _():
        o_ref[...] = acc.astype(o_ref.dtype)

    @pl.when(is_rot)
    def _():
        scale = jnp.where(j == COL_Q, HEAD_DIM ** -0.5, 1.0).astype(F32)
        cos = cos_ref[...] * scale
        sin = sin_ref[...] * scale
        lane = lax.broadcasted_iota(jnp.int32, (1, LANES), 1)
        first_half = (lane % HEAD_DIM) < (HEAD_DIM // 2)
        for c in range(acc.shape[1] // LANES):
            xs = acc[:, c * LANES:(c + 1) * LANES]
            fwd = pltpu.roll(xs, LANES - HEAD_DIM // 2, 1)
            bwd = pltpu.roll(xs, HEAD_DIM // 2, 1)
            partner = jnp.where(first_half, fwd, bwd)
            o_ref[:, c * LANES:(c + 1) * LANES] = (xs * cos + partner * sin).astype(o_ref.dtype)


def _inproj(x2, g1, w_in_bf, cos_t, sin_t, seq, tm=1024, tn=D_MODEL):
    n, d = x2.shape
    width = w_in_bf.shape[1]
    tm = min(tm, seq)
    s_blocks = seq // tm
    return pl.pallas_call(
        _inproj_kernel,
        grid=(n // tm, width // tn),
        in_specs=[
            pl.BlockSpec((tm, d), lambda i, j: (i, 0)),
            pl.BlockSpec((1, d), lambda i, j: (0, 0)),
            pl.BlockSpec((d, tn), lambda i, j: (0, j)),
            pl.BlockSpec((tm, LANES), lambda i, j: (i % s_blocks, 0)),
            pl.BlockSpec((tm, LANES), lambda i, j: (i % s_blocks, 0)),
        ],
        out_specs=pl.BlockSpec((tm, tn), lambda i, j: (i, j)),
        out_shape=jax.ShapeDtypeStruct((n, width), BF16),
        scratch_shapes=[pltpu.VMEM((tm, d), BF16)],
        compiler_params=pltpu.CompilerParams(
            dimension_semantics=("arbitrary", "arbitrary"), vmem_limit_bytes=VMEM_LIMIT),
        name="inproj",
    )(x2, g1, w_in_bf, cos_t, sin_t)


def _attn_kernel(q_ref, k_ref, v_ref, lamv_ref, g_ref, o_ref, acc_ref, *, tq, tk):
    qi = pl.program_id(2)
    q = q_ref[...]
    lane = lax.broadcasted_iota(jnp.int32, (1, LANES), 1)
    zero = jnp.zeros_like(q)
    qs = jnp.concatenate([jnp.where(lane < HEAD_DIM, q, zero),
                          jnp.where(lane >= HEAD_DIM, q, zero)], axis=0)
    rows = 2 * tq

    def step(kb, carry, masked):
        m, l = carry
        k = k_ref[pl.ds(pl.multiple_of(kb * tk, tk), tk), :]
        v = v_ref[pl.ds(pl.multiple_of(kb * tk, tk), tk), :]
        s = lax.dot_general(qs, k, (((1,), (1,)), ((), ())), preferred_element_type=F32)
        if masked:
            r = lax.broadcasted_iota(jnp.int32, (rows, tk), 0) % tq
            c = lax.broadcasted_iota(jnp.int32, (rows, tk), 1)
            s = jnp.where(c <= r, s, NEG_INF)
        m_new = jnp.maximum(m, jnp.max(s, axis=1, keepdims=True))
        alpha = jnp.exp(m - m_new)
        p = jnp.exp(s - m_new)
        l_new = alpha * l + jnp.sum(p, axis=1, keepdims=True)
        pv = jnp.dot(p.astype(BF16), v, preferred_element_type=F32)
        acc_ref[...] = acc_ref[...] * alpha + pv
        return m_new, l_new

    acc_ref[...] = jnp.zeros_like(acc_ref)
    m0 = jnp.full((rows, 1), NEG_INF, F32)
    l0 = jnp.zeros((rows, 1), F32)
    nfull = qi * (tq // tk) if tq >= tk else qi // (tk // tq)
    carry = lax.fori_loop(0, nfull, lambda kb, c: step(kb, c, False), (m0, l0))
    m, l = step(qi, carry, True)

    lv = lamv_ref[...]
    lam = (jnp.exp(jnp.sum(lv[0:1] * lv[1:2], axis=-1, keepdims=True))
           - jnp.exp(jnp.sum(lv[2:3] * lv[3:4], axis=-1, keepdims=True)) + LAM_INIT)
    o_all = acc_ref[...] / l
    o = o_all[:tq] - lam * o_all[tq:]
    o_ref[...] = (_rms(o, g_ref[...]) * (1.0 - LAM_INIT)).astype(o_ref.dtype)


def _attention(proj, lamv, subln_g, batch, seq, tq=256):
    n = proj.shape[0]
    tk = tq
    nq = seq // tq
    kern = functools.partial(_attn_kernel, tq=tq, tk=tk)
    blk = D_MODEL // LANES
    return pl.pallas_call(
        kern,
        grid=(batch, N_HEADS, nq),
        in_specs=[
            pl.BlockSpec((tq, LANES), lambda b, h, i: (b * nq + i, COL_Q * blk + h)),
            pl.BlockSpec((seq, LANES), lambda b, h, i: (b, COL_K * blk + h)),
            pl.BlockSpec((seq, LANES), lambda b, h, i: (b, COL_V * blk + h)),
            pl.BlockSpec((4, HEAD_DIM), lambda b, h, i: (0, 0)),
            pl.BlockSpec((1, V_DIM), lambda b, h, i: (0, 0)),
        ],
        out_specs=pl.BlockSpec((tq, LANES), lambda b, h, i: (b * nq + i, h)),
        out_shape=jax.ShapeDtypeStruct((n, N_HEADS * V_DIM), BF16),
        scratch_shapes=[pltpu.VMEM((2 * tq, V_DIM), F32)],
        compiler_params=pltpu.CompilerParams(
            dimension_semantics=("arbitrary", "arbitrary", "arbitrary"), vmem_limit_bytes=VMEM_LIMIT),
        name="diff_attn",
    )(proj, proj, proj, lamv, subln_g)


def _mix_kernel(cb_ref, cc_ref, cu_ref, hc_ref, hu_ref, gc_ref, ga_ref, at_ref, x_ref,
                cw_ref, wc_ref, wa_ref, wo_ref, o_ref, ext_ref, *, tm, seq):
    i = pl.program_id(0)
    halo = BF16_SUBLANES
    zin = cc_ref[...].astype(F32) * cu_ref[...].astype(F32)
    not_first = ((i * tm) % seq != 0).astype(F32)
    ext_ref[0:halo, :] = hc_ref[...].astype(F32) * hu_ref[...].astype(F32) * not_first
    ext_ref[halo:halo + tm, :] = zin
    cw = cw_ref[...]
    z = (cw[0:1] * ext_ref[halo - 2:halo - 2 + tm, :]
         + cw[1:2] * ext_ref[halo - 1:halo - 1 + tm, :]
         + cw[2:3] * zin)
    bz = (cb_ref[...].astype(F32) * z).astype(BF16)
    y_conv = jnp.dot(bz, wc_ref[...], preferred_element_type=F32)
    y_attn = jnp.dot(at_ref[...], wa_ref[...], preferred_element_type=F32)
    mixed = (jax.nn.sigmoid(gc_ref[...].astype(F32)) * y_conv
             + jax.nn.sigmoid(ga_ref[...].astype(F32)) * y_attn)
    o_ref[...] = x_ref[...] + jnp.dot(mixed.astype(BF16), wo_ref[...], preferred_element_type=F32)


def _mixer_out(proj, attn, x2, conv_w, wc, wa, wo, seq, tm=512):
    n, d = x2.shape
    halo = BF16_SUBLANES
    tm = min(tm, seq)
    hb = tm // halo
    kern = functools.partial(_mix_kernel, tm=tm, seq=seq)
    col = lambda c: pl.BlockSpec((tm, d), lambda i: (i, c))
    halo_spec = lambda c: pl.BlockSpec((halo, d), lambda i: (jnp.maximum(i * hb - 1, 0), c))
    wspec = pl.BlockSpec((d, d), lambda i: (0, 0))
    return pl.pallas_call(
        kern,
        grid=(n // tm,),
        in_specs=[col(COL_B), col(COL_C), col(COL_U), halo_spec(COL_C), halo_spec(COL_U),
                  col(COL_GC), col(COL_GA),
                  pl.BlockSpec((tm, d), lambda i: (i, 0)),
                  pl.BlockSpec((tm, d), lambda i: (i, 0)),
                  pl.BlockSpec((CONV_K, d), lambda i: (0, 0)),
                  wspec, wspec, wspec],
        out_specs=pl.BlockSpec((tm, d), lambda i: (i, 0)),
        out_shape=jax.ShapeDtypeStruct((n, d), F32),
        scratch_shapes=[pltpu.VMEM((tm + halo, d), F32)],
        compiler_params=pltpu.CompilerParams(
            dimension_semantics=("arbitrary",), vmem_limit_bytes=VMEM_LIMIT),
        name="mixer_out",
    )(proj, proj, proj, proj, proj, proj, proj, attn, x2, conv_w, wc, wa, wo)


_CANDS = [(i, j) for i in range(PEER_TOPK) for j in range(PEER_TOPK) if (i + 1) * (j + 1) <= PEER_TOPK]
_UNRANKED = 99.0


def _top16(s):
    iota = lax.broadcasted_iota(jnp.int32, s.shape, 0).astype(F32)
    rank = jnp.full(s.shape, _UNRANKED, F32)
    vals = []
    for r in range(PEER_TOPK):
        m = jnp.max(s, axis=0, keepdims=True)
        idx = jnp.min(jnp.where(s == m, iota, float(N_KEYS)), axis=0, keepdims=True)
        hit = iota == idx
        rank = jnp.where(hit, float(r), rank)
        s = jnp.where(hit, -jnp.inf, s)
        vals.append(m)
    return vals, rank


def _staircase(t1, t2):
    cand = [t1[i] + t2[jj] for (i, jj) in _CANDS]
    cidx = [float(i * PEER_TOPK + jj) for (i, jj) in _CANDS]
    cnt = [jnp.zeros_like(t1[0]) for _ in range(PEER_TOPK)]
    zsum = jnp.zeros_like(t1[0])
    top = cand[0]
    for _ in range(PEER_TOPK):
        m = functools.reduce(jnp.maximum, cand)
        idx = functools.reduce(
            jnp.minimum, [jnp.where(cv == m, ci, 1e9) for cv, ci in zip(cand, cidx)])
        for n, (i, _jj) in enumerate(_CANDS):
            hit = idx == cidx[n]
            cand[n] = jnp.where(hit, -jnp.inf, cand[n])
            cnt[i] = cnt[i] + jnp.where(hit, 1.0, 0.0)
        zsum = zsum + jnp.exp(m - top)
    return cnt, zsum


def _peer_kernel(h_ref, g2_ref, wq_ref, k1_ref, k2_ref, u_ref, vt_ref, gf_ref, o_ref,
                 xn_s, s_s, w1_s, cut_s, w2_s, rk2_s, t1_s, t2_s, cnt_s, zinv_s, ht_s, act_s, yt_s,
                 *, t, eb):
    j = pl.program_id(1)
    nchunk = t // LANES
    rows_per_blk = eb // N_KEYS

    @pl.when(j == 0)
    def _route():
        xn_s[...] = _rms(h_ref[...], g2_ref[...]).astype(BF16)
        yt_s[...] = jnp.zeros_like(yt_s)
        t1_s[...] = jnp.zeros_like(t1_s)
        t2_s[...] = jnp.zeros_like(t2_s)

        def head_body(h, _):
            qh = jnp.dot(xn_s[...], wq_ref[h], preferred_element_type=F32)
            for half, (k_ref, w_s, rk_s, t_s) in enumerate(
                    ((k1_ref, w1_s, cut_s, t1_s), (k2_ref, w2_s, rk2_s, t2_s))):
                qhp = qh[:, half * PEER_HALF:(half + 1) * PEER_HALF].astype(BF16)
                s_s[...] = lax.dot_general(k_ref[...], qhp, (((1,), (1,)), ((), ())),
                                           preferred_element_type=F32)

                def chunk_body(c, _):
                    sl = pl.ds(pl.multiple_of(c * LANES, LANES), LANES)
                    s = s_s[:, sl]
                    vals, rank = _top16(s)
                    w_s[h, :, sl] = jnp.exp(s - vals[0])
                    rk_s[h, :, sl] = rank
                    is_h = lax.broadcasted_iota(jnp.int32, (PEER_HEADS, LANES), 0) == h
                    for r in range(PEER_TOPK):
                        t_s[r, :, sl] = jnp.where(is_h, vals[r], t_s[r, :, sl])
                    return 0

                lax.fori_loop(0, nchunk, chunk_body, 0)
            return 0

        lax.fori_loop(0, PEER_HEADS, head_body, 0)

        def cand_body(c, _):
            sl = pl.ds(pl.multiple_of(c * LANES, LANES), LANES)
            t1 = [t1_s[i, :, sl] for i in range(PEER_TOPK)]
            t2 = [t2_s[i, :, sl] for i in range(PEER_TOPK)]
            cnt, zsum = _staircase(t1, t2)
            for i in range(PEER_TOPK):
                cnt_s[i, :, sl] = cnt[i]
            zinv_s[:, sl] = 1.0 / zsum
            return 0

        lax.fori_loop(0, nchunk, cand_body, 0)

        def fin_body(c, _):
            sl = pl.ds(pl.multiple_of(c * LANES, LANES), LANES)
            for h in range(PEER_HEADS):
                rank1 = cut_s[h, :, sl]
                cut = jnp.zeros_like(rank1)
                for i in range(PEER_TOPK):
                    cut = jnp.where(rank1 == float(i), cnt_s[i, h:h + 1, sl], cut)
                cut_s[h, :, sl] = cut
                w1_s[h, :, sl] = w1_s[h, :, sl] * zinv_s[h:h + 1, sl]
            return 0

        lax.fori_loop(0, nchunk, fin_body, 0)

    ht_s[...] = lax.dot_general(u_ref[...], xn_s[...], (((1,), (1,)), ((), ())),
                                preferred_element_type=F32)

    grp = pl.ds(pl.multiple_of(j * rows_per_blk, rows_per_blk), rows_per_blk)

    def chunk_body(c, _):
        sl = pl.ds(pl.multiple_of(c * LANES, LANES), LANES)
        cuts = [cut_s[h, grp, sl] for h in range(PEER_HEADS)]
        w1s = [w1_s[h, grp, sl] for h in range(PEER_HEADS)]
        for rr in range(rows_per_blk):
            rsl = slice(rr * N_KEYS, (rr + 1) * N_KEYS)
            gate = jnp.zeros((N_KEYS, LANES), F32)
            for h in range(PEER_HEADS):
                keep = rk2_s[h, :, sl] < cuts[h][rr:rr + 1]
                gate = gate + jnp.where(keep, w2_s[h, :, sl], 0.0) * w1s[h][rr:rr + 1]
            hh = ht_s[rsl, sl]
            act = 0.5 * hh * (1.0 + lax.erf(hh * (2.0 ** -0.5))) * gate
            act_s[rsl, sl] = act.astype(BF16)
        return 0

    lax.fori_loop(0, nchunk, chunk_body, 0)
    yt_s[...] += jnp.dot(vt_ref[...], act_s[...], preferred_element_type=F32)

    @pl.when(j == pl.num_programs(1) - 1)
    def _():
        h2 = h_ref[...] + yt_s[...].T
        o_ref[...] = _rms(h2, gf_ref[...])


def _peer(h1, g2, wq3, k1, k2, u_bf, vt_bf, gf, t=512, eb=1024):
    n, d = h1.shape
    n_exp = u_bf.shape[0]
    kern = functools.partial(_peer_kernel, t=t, eb=eb)
    hk = (PEER_HEADS, N_KEYS, t)
    return pl.pallas_call(
        kern,
        grid=(n // t, n_exp // eb),
        in_specs=[
            pl.BlockSpec((t, d), lambda i, j: (i, 0)),
            pl.BlockSpec((1, d), lambda i, j: (0, 0)),
            pl.BlockSpec(wq3.shape, lambda i, j: (0, 0, 0)),
            pl.BlockSpec(k1.shape, lambda i, j: (0, 0)),
            pl.BlockSpec(k2.shape, lambda i, j: (0, 0)),
            pl.BlockSpec((eb, d), lambda i, j: (j, 0)),
            pl.BlockSpec((d, eb), lambda i, j: (0, j)),
            pl.BlockSpec((1, d), lambda i, j: (0, 0)),
        ],
        out_specs=pl.BlockSpec((t, d), lambda i, j: (i, 0)),
        out_shape=jax.ShapeDtypeStruct((n, d), F32),
        scratch_shapes=[
            pltpu.VMEM((t, d), BF16),
            pltpu.VMEM((N_KEYS, t), F32),
            pltpu.VMEM(hk, F32),
            pltpu.VMEM(hk, F32),
            pltpu.VMEM(hk, F32),
            pltpu.VMEM(hk, F32),
            pltpu.VMEM((PEER_TOPK, PEER_HEADS, t), F32),
            pltpu.VMEM((PEER_TOPK, PEER_HEADS, t), F32),
            pltpu.VMEM((PEER_TOPK, PEER_HEADS, t), F32),
            pltpu.VMEM((PEER_HEADS, t), F32),
            pltpu.VMEM((eb, t), F32),
            pltpu.VMEM((eb, t), BF16),
            pltpu.VMEM((d, t), F32),
        ],
        compiler_params=pltpu.CompilerParams(
            dimension_semantics=("arbitrary", "arbitrary"), vmem_limit_bytes=VMEM_LIMIT),
        name="peer",
    )(h1, g2, wq3, k1, k2, u_bf, vt_bf, gf)


def _rope_tables(seq):
    half = HEAD_DIM // 2
    inv = ROPE_THETA ** (-jnp.arange(0, HEAD_DIM, 2, dtype=F32) / HEAD_DIM)
    ang = jnp.arange(seq, dtype=F32)[:, None] * inv[None, :]
    cos, sin = jnp.cos(ang), jnp.sin(ang)
    reps = LANES // half
    cos_t = jnp.tile(cos, (1, reps))
    sin_t = jnp.tile(jnp.concatenate([-sin, sin], axis=1), (1, reps // 2))
    return cos_t, sin_t


def kernel(x, norm1_g, w_in, conv_w, w_out_conv, lambda_q1, lambda_k1, lambda_q2, lambda_k2,
           attn_subln_g, w_out_attn, w_o, norm2_g, peer_w_q, peer_keys1, peer_keys2,
           peer_u, peer_v, final_norm_g):
    batch, seq, d = x.shape
    assert d == D_MODEL and norm1_g.shape[0] == 1
    x2 = x.reshape(batch * seq, d)
    cos_t, sin_t = _rope_tables(seq)
    proj = _inproj(x2, norm1_g[0][None], w_in[0].astype(BF16), cos_t, sin_t, seq)
    lamv = jnp.stack([lambda_q1[0], lambda_k1[0], lambda_q2[0], lambda_k2[0]]).astype(F32)
    attn = _attention(proj, lamv, attn_subln_g[0][None].astype(F32), batch, seq)
    h1 = _mixer_out(proj, attn, x2, conv_w[0], w_out_conv[0].astype(BF16),
                    w_out_attn[0].astype(BF16), w_o[0].astype(BF16), seq)
    wq3 = peer_w_q[0].astype(BF16).reshape(d, PEER_HEADS, 2 * PEER_HALF).transpose(1, 0, 2)
    out = _peer(h1, norm2_g[0][None], wq3, peer_keys1[0].astype(BF16), peer_keys2[0].astype(BF16),
                peer_u[0].astype(BF16), peer_v[0].astype(BF16).T, final_norm_g[None])
    return out.reshape(batch, seq, d)
```

```python
import functools
import math

import jax
import jax.numpy as jnp
from jax import lax
from jax.experimental import pallas as pl
from jax.experimental.pallas import tpu as pltpu

F32 = jnp.float32
BF16 = jnp.bfloat16

D_MODEL = 1024
N_HEADS = 8
HEAD_DIM = 64
V_DIM = 2 * HEAD_DIM
CONV_K = 3
ROPE_THETA = 10000.0
RMS_EPS = 1e-6
LAM_INIT = 0.8 - 0.6 * math.exp(-0.3 * 0)
NEG_INF = -1e30

PEER_HEADS = 8
PEER_HALF = 128
N_KEYS = 128
PEER_TOPK = 16

LANES = 128
BF16_SUBLANES = 16

COL_B, COL_C, COL_U, COL_Q, COL_K, COL_V, COL_GC, COL_GA = range(8)

VMEM_LIMIT = 56 * 1024 * 1024


def _rms(x, g):
    ms = jnp.mean(x * x, axis=-1, keepdims=True)
    return x * lax.rsqrt(ms + RMS_EPS) * g


def _inproj_kernel(x_ref, g_ref, w_ref, cos_ref, sin_ref, o_ref, xn_ref):
    j = pl.program_id(1)

    @pl.when(j == 0)
    def _():
        xn_ref[...] = _rms(x_ref[...], g_ref[...]).astype(BF16)

    acc = jnp.dot(xn_ref[...], w_ref[...], preferred_element_type=F32)
    is_rot = jnp.logical_or(j == COL_Q, j == COL_K)

    @pl.when(jnp.logical_not(is_rot))
    def _():
        o_ref[...] = acc.astype(o_ref.dtype)

    @pl.when(is_rot)
    def _():
        scale = jnp.where(j == COL_Q, HEAD_DIM ** -0.5 * math.log2(math.e), 1.0).astype(F32)
        cos = cos_ref[...] * scale
        sin = sin_ref[...] * scale
        lane = lax.broadcasted_iota(jnp.int32, (1, LANES), 1)
        first_half = (lane % HEAD_DIM) < (HEAD_DIM // 2)
        for c in range(acc.shape[1] // LANES):
            xs = acc[:, c * LANES:(c + 1) * LANES]
            fwd = pltpu.roll(xs, LANES - HEAD_DIM // 2, 1)
            bwd = pltpu.roll(xs, HEAD_DIM // 2, 1)
            partner = jnp.where(first_half, fwd, bwd)
            o_ref[:, c * LANES:(c + 1) * LANES] = (xs * cos + partner * sin).astype(o_ref.dtype)


def _inproj(x2, g1, w_in_bf, cos_t, sin_t, seq, tm=1024, tn=D_MODEL):
    n, d = x2.shape
    width = w_in_bf.shape[1]
    tm = min(tm, seq)
    s_blocks = seq // tm
    return pl.pallas_call(
        _inproj_kernel,
        grid=(n // tm, width // tn),
        in_specs=[
            pl.BlockSpec((tm, d), lambda i, j: (i, 0)),
            pl.BlockSpec((1, d), lambda i, j: (0, 0)),
            pl.BlockSpec((d, tn), lambda i, j: (0, j)),
            pl.BlockSpec((tm, LANES), lambda i, j: (i % s_blocks, 0)),
            pl.BlockSpec((tm, LANES), lambda i, j: (i % s_blocks, 0)),
        ],
        out_specs=pl.BlockSpec((tm, tn), lambda i, j: (i, j)),
        out_shape=jax.ShapeDtypeStruct((n, width), BF16),
        scratch_shapes=[pltpu.VMEM((tm, d), BF16)],
        compiler_params=pltpu.CompilerParams(
            dimension_semantics=("arbitrary", "arbitrary"), vmem_limit_bytes=VMEM_LIMIT),
        name="inproj",
    )(x2, g1, w_in_bf, cos_t, sin_t)


def _attn_kernel(q_ref, k_ref, v_ref, lamv_ref, g_ref, o_ref, acc_ref, *, tq, tk):
    qi = pl.program_id(2)
    q = q_ref[...]
    lane = lax.broadcasted_iota(jnp.int32, (1, LANES), 1)
    zero = jnp.zeros_like(q)
    qs = jnp.concatenate([jnp.where(lane < HEAD_DIM, q, zero),
                          jnp.where(lane >= HEAD_DIM, q, zero)], axis=0)
    rows = 2 * tq

    def step(kb, carry, masked):
        m, l = carry
        k = k_ref[pl.ds(pl.multiple_of(kb * tk, tk), tk), :]
        v = v_ref[pl.ds(pl.multiple_of(kb * tk, tk), tk), :]
        s = lax.dot_general(qs, k, (((1,), (1,)), ((), ())), preferred_element_type=F32)
        if masked:
            r = lax.broadcasted_iota(jnp.int32, (tq, tk), 0) + qi * tq
            c = lax.broadcasted_iota(jnp.int32, (tq, tk), 1) + kb * tk
            ok = c <= r
            s = jnp.concatenate([jnp.where(ok, s[:tq], NEG_INF), jnp.where(ok, s[tq:], NEG_INF)], axis=0)
        m_new = jnp.maximum(m, jnp.max(s, axis=1, keepdims=True))
        alpha = jnp.exp2(m - m_new)
        p = jnp.exp2(s - m_new)
        l_new = alpha * l + jnp.sum(p, axis=1, keepdims=True)
        pv = jnp.dot(p.astype(BF16), v, preferred_element_type=F32)
        acc_ref[...] = acc_ref[...] * alpha + pv
        return m_new, l_new

    acc_ref[...] = jnp.zeros_like(acc_ref)
    m0 = jnp.full((rows, 1), NEG_INF, F32)
    l0 = jnp.zeros((rows, 1), F32)
    nfull = (qi * tq) // tk
    carry = lax.fori_loop(0, nfull, lambda kb, c: step(kb, c, False), (m0, l0))
    m, l = step(nfull, carry, True)

    lv = lamv_ref[...]
    lam = (jnp.exp(jnp.sum(lv[0:1] * lv[1:2], axis=-1, keepdims=True))
           - jnp.exp(jnp.sum(lv[2:3] * lv[3:4], axis=-1, keepdims=True)) + LAM_INIT)
    o_all = acc_ref[...] / l
    o = o_all[:tq] - lam * o_all[tq:]
    o_ref[...] = (_rms(o, g_ref[...]) * (1.0 - LAM_INIT)).astype(o_ref.dtype)


def _attention(proj, lamv, subln_g, batch, seq, tq=256, tk=1024):
    n = proj.shape[0]
    tq, tk = min(tq, seq), min(tk, seq)
    assert tk % tq == 0 and seq % tk == 0
    nq = seq // tq
    kern = functools.partial(_attn_kernel, tq=tq, tk=tk)
    blk = D_MODEL // LANES
    return pl.pallas_call(
        kern,
        grid=(batch, N_HEADS, nq),
        in_specs=[
            pl.BlockSpec((tq, LANES), lambda b, h, i: (b * nq + i, COL_Q * blk + h)),
            pl.BlockSpec((seq, LANES), lambda b, h, i: (b, COL_K * blk + h)),
            pl.BlockSpec((seq, LANES), lambda b, h, i: (b, COL_V * blk + h)),
            pl.BlockSpec((4, HEAD_DIM), lambda b, h, i: (0, 0)),
            pl.BlockSpec((1, V_DIM), lambda b, h, i: (0, 0)),
        ],
        out_specs=pl.BlockSpec((tq, LANES), lambda b, h, i: (b * nq + i, h)),
        out_shape=jax.ShapeDtypeStruct((n, N_HEADS * V_DIM), BF16),
        scratch_shapes=[pltpu.VMEM((2 * tq, V_DIM), F32)],
        compiler_params=pltpu.CompilerParams(
            dimension_semantics=("arbitrary", "arbitrary", "arbitrary"), vmem_limit_bytes=VMEM_LIMIT),
        name="diff_attn",
    )(proj, proj, proj, lamv, subln_g)


def _mix_kernel(cb_ref, cc_ref, cu_ref, hc_ref, hu_ref, gc_ref, ga_ref, at_ref, x_ref,
                cw_ref, wc_ref, wa_ref, wo_ref, o_ref, ext_ref, *, tm, seq):
    i = pl.program_id(0)
    halo = BF16_SUBLANES
    zin = cc_ref[...].astype(F32) * cu_ref[...].astype(F32)
    not_first = ((i * tm) % seq != 0).astype(F32)
    ext_ref[0:halo, :] = hc_ref[...].astype(F32) * hu_ref[...].astype(F32) * not_first
    ext_ref[halo:halo + tm, :] = zin
    cw = cw_ref[...]
    z = (cw[0:1] * ext_ref[halo - 2:halo - 2 + tm, :]
         + cw[1:2] * ext_ref[halo - 1:halo - 1 + tm, :]
         + cw[2:3] * zin)
    bz = (cb_ref[...].astype(F32) * z).astype(BF16)
    y_conv = jnp.dot(bz, wc_ref[...], preferred_element_type=F32)
    y_attn = jnp.dot(at_ref[...], wa_ref[...], preferred_element_type=F32)
    mixed = (jax.nn.sigmoid(gc_ref[...].astype(F32)) * y_conv
             + jax.nn.sigmoid(ga_ref[...].astype(F32)) * y_attn)
    o_ref[...] = x_ref[...] + jnp.dot(mixed.astype(BF16), wo_ref[...], preferred_element_type=F32)


def _mixer_out(proj, attn, x2, conv_w, wc, wa, wo, seq, tm=512):
    n, d = x2.shape
    halo = BF16_SUBLANES
    tm = min(tm, seq)
    hb = tm // halo
    kern = functools.partial(_mix_kernel, tm=tm, seq=seq)
    col = lambda c: pl.BlockSpec((tm, d), lambda i: (i, c))
    halo_spec = lambda c: pl.BlockSpec((halo, d), lambda i: (jnp.maximum(i * hb - 1, 0), c))
    wspec = pl.BlockSpec((d, d), lambda i: (0, 0))
    return pl.pallas_call(
        kern,
        grid=(n // tm,),
        in_specs=[col(COL_B), col(COL_C), col(COL_U), halo_spec(COL_C), halo_spec(COL_U),
                  col(COL_GC), col(COL_GA),
                  pl.BlockSpec((tm, d), lambda i: (i, 0)),
                  pl.BlockSpec((tm, d), lambda i: (i, 0)),
                  pl.BlockSpec((CONV_K, d), lambda i: (0, 0)),
                  wspec, wspec, wspec],
        out_specs=pl.BlockSpec((tm, d), lambda i: (i, 0)),
        out_shape=jax.ShapeDtypeStruct((n, d), F32),
        scratch_shapes=[pltpu.VMEM((tm + halo, d), F32)],
        compiler_params=pltpu.CompilerParams(
            dimension_semantics=("arbitrary",), vmem_limit_bytes=VMEM_LIMIT),
        name="mixer_out",
    )(proj, proj, proj, proj, proj, proj, proj, attn, x2, conv_w, wc, wa, wo)


_CANDS = [(i, j) for i in range(PEER_TOPK) for j in range(PEER_TOPK) if (i + 1) * (j + 1) <= PEER_TOPK]
_UNRANKED = 99.0
_RANK_MARK = 2.0 ** 100
SWEEP_KEYS = 2


def _top16(s):
    iota = lax.broadcasted_iota(jnp.int32, s.shape, 0).astype(F32)
    vals = []
    for r in range(PEER_TOPK):
        m = jnp.max(s, axis=0, keepdims=True)
        idx = jnp.min(jnp.where(s == m, iota, float(N_KEYS)), axis=0, keepdims=True)
        s = jnp.where(iota == idx, -_RANK_MARK * (r + 1), s)
        vals.append(m)
    rank = jnp.where(s <= -_RANK_MARK, s * (-1.0 / _RANK_MARK) - 1.0, _UNRANKED)
    return vals, rank


def _staircase(t1, t2):
    cand = [t1[i] + t2[jj] for (i, jj) in _CANDS]
    cidx = [float(i * PEER_TOPK + jj) for (i, jj) in _CANDS]
    cnt = [jnp.zeros_like(t1[0]) for _ in range(PEER_TOPK)]
    zsum = jnp.zeros_like(t1[0])
    top = cand[0]
    for _ in range(PEER_TOPK):
        m = functools.reduce(jnp.maximum, cand)
        idx = functools.reduce(
            jnp.minimum, [jnp.where(cv == m, ci, 1e9) for cv, ci in zip(cand, cidx)])
        for n, (i, _jj) in enumerate(_CANDS):
            hit = idx == cidx[n]
            cand[n] = jnp.where(hit, -jnp.inf, cand[n])
            cnt[i] = cnt[i] + jnp.where(hit, 1.0, 0.0)
        zsum = zsum + jnp.exp(m - top)
    return cnt, zsum


def _peer_kernel(h_ref, g2_ref, wq_ref, k1_ref, k2_ref, u_ref, vt_ref, gf_ref, o_ref,
                 xn_s, s_s, w1_s, cut_s, w2_s, rk2_s, t1_s, t2_s, cnt_s, zinv_s, act_s, yt_s,
                 *, t, eb):
    j = pl.program_id(1)
    nchunk = t // LANES
    rows_per_blk = eb // N_KEYS

    @pl.when(j == 0)
    def _route():
        xn_s[...] = _rms(h_ref[...], g2_ref[...]).astype(BF16)
        yt_s[...] = jnp.zeros_like(yt_s)
        t1_s[...] = jnp.zeros_like(t1_s)
        t2_s[...] = jnp.zeros_like(t2_s)

        def head_body(h, _):
            qh = jnp.dot(xn_s[...], wq_ref[h], preferred_element_type=F32)
            halves = ((k1_ref, w1_s, cut_s, t1_s), (k2_ref, w2_s, rk2_s, t2_s))
            for half, (k_ref, _, _, _) in enumerate(halves):
                qhp = qh[:, half * PEER_HALF:(half + 1) * PEER_HALF].astype(BF16)
                s_s[half] = lax.dot_general(k_ref[...], qhp, (((1,), (1,)), ((), ())),
                                            preferred_element_type=F32)

            def chunk_body(c, _):
                sl = pl.ds(pl.multiple_of(c * LANES, LANES), LANES)
                is_h = lax.broadcasted_iota(jnp.int32, (PEER_HEADS, LANES), 0) == h
                for half, (_, w_s, rk_s, t_s) in enumerate(halves):
                    s = s_s[half, :, sl]
                    vals, rank = _top16(s)
                    w_s[h, :, sl] = jnp.exp(s - vals[0])
                    rk_s[h, :, sl] = rank
                    for r in range(PEER_TOPK):
                        t_s[r, :, sl] = jnp.where(is_h, vals[r], t_s[r, :, sl])
                return 0

            lax.fori_loop(0, nchunk, chunk_body, 0)
            return 0

        lax.fori_loop(0, PEER_HEADS, head_body, 0)

        def cand_body(c, _):
            sl = pl.ds(pl.multiple_of(c * LANES, LANES), LANES)
            t1 = [t1_s[i, :, sl] for i in range(PEER_TOPK)]
            t2 = [t2_s[i, :, sl] for i in range(PEER_TOPK)]
            cnt, zsum = _staircase(t1, t2)
            for i in range(PEER_TOPK):
                cnt_s[i, :, sl] = cnt[i]
            zinv_s[:, sl] = 1.0 / zsum
            return 0

        lax.fori_loop(0, nchunk, cand_body, 0)

        def fin_body(c, _):
            sl = pl.ds(pl.multiple_of(c * LANES, LANES), LANES)
            for h in range(PEER_HEADS):
                rank1 = cut_s[h, :, sl]
                cut = jnp.zeros_like(rank1)
                for i in range(PEER_TOPK):
                    cut = jnp.where(rank1 == float(i), cnt_s[i, h:h + 1, sl], cut)
                cut_s[h, :, sl] = cut
                w1_s[h, :, sl] = w1_s[h, :, sl] * zinv_s[h:h + 1, sl]
            return 0

        lax.fori_loop(0, nchunk, fin_body, 0)

    grp = pl.ds(pl.multiple_of(j * rows_per_blk, rows_per_blk), rows_per_blk)
    gk = SWEEP_KEYS * N_KEYS
    acc = yt_s[...]
    for g in range(eb // gk):
        ht = lax.dot_general(u_ref[g * gk:(g + 1) * gk, :], xn_s[...], (((1,), (1,)), ((), ())),
                             preferred_element_type=F32)
        for r2 in range(SWEEP_KEYS):
            rr = g * SWEEP_KEYS + r2
            for c in range(nchunk):
                sl = slice(c * LANES, (c + 1) * LANES)
                gate = jnp.zeros((N_KEYS, LANES), F32)
                for h in range(PEER_HEADS):
                    keep = rk2_s[h, :, sl] < cut_s[h, grp, sl][rr:rr + 1]
                    gate = gate + jnp.where(keep, w2_s[h, :, sl], 0.0) * w1_s[h, grp, sl][rr:rr + 1]
                hh = ht[r2 * N_KEYS:(r2 + 1) * N_KEYS, sl]
                act = 0.5 * hh * (1.0 + lax.erf(hh * (2.0 ** -0.5))) * gate
                act_s[rr * N_KEYS:(rr + 1) * N_KEYS, sl] = act.astype(BF16)
        acc = acc + jnp.dot(vt_ref[:, g * gk:(g + 1) * gk], act_s[g * gk:(g + 1) * gk, :],
                            preferred_element_type=F32)
    yt_s[...] = acc

    @pl.when(j == pl.num_programs(1) - 1)
    def _():
        h2 = h_ref[...] + yt_s[...].T
        o_ref[...] = _rms(h2, gf_ref[...])


def _peer(h1, g2, wq3, k1, k2, u_bf, vt_bf, gf, t=512, eb=1024):
    n, d = h1.shape
    n_exp = u_bf.shape[0]
    kern = functools.partial(_peer_kernel, t=t, eb=eb)
    hk = (PEER_HEADS, N_KEYS, t)
    return pl.pallas_call(
        kern,
        grid=(n // t, n_exp // eb),
        in_specs=[
            pl.BlockSpec((t, d), lambda i, j: (i, 0)),
            pl.BlockSpec((1, d), lambda i, j: (0, 0)),
            pl.BlockSpec(wq3.shape, lambda i, j: (0, 0, 0)),
            pl.BlockSpec(k1.shape, lambda i, j: (0, 0)),
            pl.BlockSpec(k2.shape, lambda i, j: (0, 0)),
            pl.BlockSpec((eb, d), lambda i, j: (j, 0)),
            pl.BlockSpec((d, eb), lambda i, j: (0, j)),
            pl.BlockSpec((1, d), lambda i, j: (0, 0)),
        ],
        out_specs=pl.BlockSpec((t, d), lambda i, j: (i, 0)),
        out_shape=jax.ShapeDtypeStruct((n, d), F32),
        scratch_shapes=[
            pltpu.VMEM((t, d), BF16),
            pltpu.VMEM((2, N_KEYS, t), F32),
            pltpu.VMEM(hk, F32),
            pltpu.VMEM(hk, F32),
            pltpu.VMEM(hk, F32),
            pltpu.VMEM(hk, F32),
            pltpu.VMEM((PEER_TOPK, PEER_HEADS, t), F32),
            pltpu.VMEM((PEER_TOPK, PEER_HEADS, t), F32),
            pltpu.VMEM((PEER_TOPK, PEER_HEADS, t), F32),
            pltpu.VMEM((PEER_HEADS, t), F32),
            pltpu.VMEM((eb, t), BF16),
            pltpu.VMEM((d, t), F32),
        ],
        compiler_params=pltpu.CompilerParams(
            dimension_semantics=("arbitrary", "arbitrary"), vmem_limit_bytes=VMEM_LIMIT),
        name="peer",
    )(h1, g2, wq3, k1, k2, u_bf, vt_bf, gf)


def _rope_tables(seq):
    half = HEAD_DIM // 2
    inv = ROPE_THETA ** (-jnp.arange(0, HEAD_DIM, 2, dtype=F32) / HEAD_DIM)
    ang = jnp.arange(seq, dtype=F32)[:, None] * inv[None, :]
    cos, sin = jnp.cos(ang), jnp.sin(ang)
    reps = LANES // half
    cos_t = jnp.tile(cos, (1, reps))
    sin_t = jnp.tile(jnp.concatenate([-sin, sin], axis=1), (1, reps // 2))
    return cos_t, sin_t


def kernel(x, norm1_g, w_in, conv_w, w_out_conv, lambda_q1, lambda_k1, lambda_q2, lambda_k2,
           attn_subln_g, w_out_attn, w_o, norm2_g, peer_w_q, peer_keys1, peer_keys2,
           peer_u, peer_v, final_norm_g):
    batch, seq, d = x.shape
    assert d == D_MODEL and norm1_g.shape[0] == 1
    x2 = x.reshape(batch * seq, d)
    cos_t, sin_t = _rope_tables(seq)
    proj = _inproj(x2, norm1_g[0][None], w_in[0].astype(BF16), cos_t, sin_t, seq)
    lamv = jnp.stack([lambda_q1[0], lambda_k1[0], lambda_q2[0], lambda_k2[0]]).astype(F32)
    attn = _attention(proj, lamv, attn_subln_g[0][None].astype(F32), batch, seq)
    h1 = _mixer_out(proj, attn, x2, conv_w[0], w_out_conv[0].astype(BF16),
                    w_out_attn[0].astype(BF16), w_o[0].astype(BF16), seq)
    wq3 = peer_w_q[0].astype(BF16).reshape(d, PEER_HEADS, 2 * PEER_HALF).transpose(1, 0, 2)
    out = _peer(h1, norm2_g[0][None], wq3, peer_keys1[0].astype(BF16), peer_keys2[0].astype(BF16),
                peer_u[0].astype(BF16), peer_v[0].astype(BF16).T, final_norm_g[None])
    return out.reshape(batch, seq, d)
```

```python
import functools
import math

import jax
import jax.numpy as jnp
from jax import lax
from jax.experimental import pallas as pl
from jax.experimental.pallas import tpu as pltpu

F32 = jnp.float32
BF16 = jnp.bfloat16

D_MODEL = 1024
N_HEADS = 8
HEAD_DIM = 64
V_DIM = 2 * HEAD_DIM
CONV_K = 3
ROPE_THETA = 10000.0
RMS_EPS = 1e-6
LAM_INIT = 0.8 - 0.6 * math.exp(-0.3 * 0)
NEG_INF = -1e30

PEER_HEADS = 8
PEER_HALF = 128
N_KEYS = 128
PEER_TOPK = 16

LANES = 128
BF16_SUBLANES = 16

COL_B, COL_C, COL_U, COL_Q, COL_K, COL_V, COL_GC, COL_GA = range(8)

VMEM_LIMIT = 56 * 1024 * 1024


def _rms(x, g):
    ms = jnp.mean(x * x, axis=-1, keepdims=True)
    return x * lax.rsqrt(ms + RMS_EPS) * g


def _inproj_kernel(x_ref, g_ref, w_ref, cos_ref, sin_ref, o_ref, xn_ref):
    j = pl.program_id(1)

    @pl.when(j == 0)
    def _():
        xn_ref[...] = _rms(x_ref[...], g_ref[...]).astype(BF16)

    acc = jnp.dot(xn_ref[...], w_ref[...], preferred_element_type=F32)
    is_rot = jnp.logical_or(j == COL_Q, j == COL_K)

    @pl.when(jnp.logical_not(is_rot))
    def _():
        o_ref[...] = acc.astype(o_ref.dtype)

    @pl.when(is_rot)
    def _():
        scale = jnp.where(j == COL_Q, HEAD_DIM ** -0.5 * math.log2(math.e), 1.0).astype(F32)
        cos = cos_ref[...] * scale
        sin = sin_ref[...] * scale
        lane = lax.broadcasted_iota(jnp.int32, (1, LANES), 1)
        first_half = (lane % HEAD_DIM) < (HEAD_DIM // 2)
        for c in range(acc.shape[1] // LANES):
            xs = acc[:, c * LANES:(c + 1) * LANES]
            fwd = pltpu.roll(xs, LANES - HEAD_DIM // 2, 1)
            bwd = pltpu.roll(xs, HEAD_DIM // 2, 1)
            partner = jnp.where(first_half, fwd, bwd)
            o_ref[:, c * LANES:(c + 1) * LANES] = (xs * cos + partner * sin).astype(o_ref.dtype)


def _inproj(x2, g1, w_in_bf, cos_t, sin_t, seq, tm=1024, tn=D_MODEL):
    n, d = x2.shape
    width = w_in_bf.shape[1]
    tm = min(tm, seq)
    s_blocks = seq // tm
    return pl.pallas_call(
        _inproj_kernel,
        grid=(n // tm, width // tn),
        in_specs=[
            pl.BlockSpec((tm, d), lambda i, j: (i, 0)),
            pl.BlockSpec((1, d), lambda i, j: (0, 0)),
            pl.BlockSpec((d, tn), lambda i, j: (0, j)),
            pl.BlockSpec((tm, LANES), lambda i, j: (i % s_blocks, 0)),
            pl.BlockSpec((tm, LANES), lambda i, j: (i % s_blocks, 0)),
        ],
        out_specs=pl.BlockSpec((tm, tn), lambda i, j: (i, j)),
        out_shape=jax.ShapeDtypeStruct((n, width), BF16),
        scratch_shapes=[pltpu.VMEM((tm, d), BF16)],
        compiler_params=pltpu.CompilerParams(
            dimension_semantics=("arbitrary", "arbitrary"), vmem_limit_bytes=VMEM_LIMIT),
        name="inproj",
    )(x2, g1, w_in_bf, cos_t, sin_t)


ATTN_HEADS_PER_STEP = 2


def _attn_kernel(q_ref, k_ref, v_ref, lamv_ref, g_ref, o_ref, acc_ref, *, tq, tk):
    qi = pl.program_id(2)
    rows = 2 * tq
    lane = lax.broadcasted_iota(jnp.int32, (1, LANES), 1)
    heads = range(ATTN_HEADS_PER_STEP)
    hsl = [slice(a * LANES, (a + 1) * LANES) for a in heads]

    def stacked(q):
        zero = jnp.zeros_like(q)
        return jnp.concatenate([jnp.where(lane < HEAD_DIM, q, zero),
                                jnp.where(lane >= HEAD_DIM, q, zero)], axis=0)

    qs = [stacked(q_ref[:, hsl[a]]) for a in heads]

    def step(kb, carry, masked):
        ksl = pl.ds(pl.multiple_of(kb * tk, tk), tk)
        if masked:
            r = lax.broadcasted_iota(jnp.int32, (tq, tk), 0) + qi * tq
            c = lax.broadcasted_iota(jnp.int32, (tq, tk), 1) + kb * tk
            ok = c <= r
        out = []
        for a in heads:
            m, l = carry[a]
            s = lax.dot_general(qs[a], k_ref[ksl, hsl[a]], (((1,), (1,)), ((), ())),
                                preferred_element_type=F32)
            if masked:
                s = jnp.concatenate([jnp.where(ok, s[:tq], NEG_INF), jnp.where(ok, s[tq:], NEG_INF)], axis=0)
            m_new = jnp.maximum(m, jnp.max(s, axis=1, keepdims=True))
            alpha = jnp.exp2(m - m_new)
            p = jnp.exp2(s - m_new)
            l_new = alpha * l + jnp.sum(p, axis=1, keepdims=True)
            pv = jnp.dot(p.astype(BF16), v_ref[ksl, hsl[a]], preferred_element_type=F32)
            acc_ref[a] = acc_ref[a] * alpha + pv
            out.append((m_new, l_new))
        return tuple(out)

    acc_ref[...] = jnp.zeros_like(acc_ref)
    init = tuple((jnp.full((rows, 1), NEG_INF, F32), jnp.zeros((rows, 1), F32)) for _ in heads)
    nfull = (qi * tq) // tk
    carry = lax.fori_loop(0, nfull, lambda kb, c: step(kb, c, False), init)
    carry = step(nfull, carry, True)

    lv = lamv_ref[...]
    lam = (jnp.exp(jnp.sum(lv[0:1] * lv[1:2], axis=-1, keepdims=True))
           - jnp.exp(jnp.sum(lv[2:3] * lv[3:4], axis=-1, keepdims=True)) + LAM_INIT)
    for a in heads:
        o_all = acc_ref[a] / carry[a][1]
        o = o_all[:tq] - lam * o_all[tq:]
        o_ref[:, hsl[a]] = (_rms(o, g_ref[...]) * (1.0 - LAM_INIT)).astype(o_ref.dtype)


def _attention(proj, lamv, subln_g, batch, seq, tq=256, tk=1024):
    n = proj.shape[0]
    tq, tk = min(tq, seq), min(tk, seq)
    assert tk % tq == 0 and seq % tk == 0
    nq = seq // tq
    hw = ATTN_HEADS_PER_STEP * LANES
    kern = functools.partial(_attn_kernel, tq=tq, tk=tk)
    blk = D_MODEL // hw
    return pl.pallas_call(
        kern,
        grid=(batch, N_HEADS // ATTN_HEADS_PER_STEP, nq),
        in_specs=[
            pl.BlockSpec((tq, hw), lambda b, h, i: (b * nq + i, COL_Q * blk + h)),
            pl.BlockSpec((seq, hw), lambda b, h, i: (b, COL_K * blk + h)),
            pl.BlockSpec((seq, hw), lambda b, h, i: (b, COL_V * blk + h)),
            pl.BlockSpec((4, HEAD_DIM), lambda b, h, i: (0, 0)),
            pl.BlockSpec((1, V_DIM), lambda b, h, i: (0, 0)),
        ],
        out_specs=pl.BlockSpec((tq, hw), lambda b, h, i: (b * nq + i, h)),
        out_shape=jax.ShapeDtypeStruct((n, N_HEADS * V_DIM), BF16),
        scratch_shapes=[pltpu.VMEM((ATTN_HEADS_PER_STEP, 2 * tq, V_DIM), F32)],
        compiler_params=pltpu.CompilerParams(
            dimension_semantics=("arbitrary", "arbitrary", "arbitrary"), vmem_limit_bytes=VMEM_LIMIT),
        name="diff_attn",
    )(proj, proj, proj, lamv, subln_g)


def _mix_kernel(cb_ref, cc_ref, cu_ref, hc_ref, hu_ref, gc_ref, ga_ref, at_ref, x_ref,
                cw_ref, wc_ref, wa_ref, wo_ref, o_ref, ext_ref, *, tm, seq):
    i = pl.program_id(0)
    halo = BF16_SUBLANES
    zin = cc_ref[...].astype(F32) * cu_ref[...].astype(F32)
    not_first = ((i * tm) % seq != 0).astype(F32)
    ext_ref[0:halo, :] = hc_ref[...].astype(F32) * hu_ref[...].astype(F32) * not_first
    ext_ref[halo:halo + tm, :] = zin
    cw = cw_ref[...]
    z = (cw[0:1] * ext_ref[halo - 2:halo - 2 + tm, :]
         + cw[1:2] * ext_ref[halo - 1:halo - 1 + tm, :]
         + cw[2:3] * zin)
    bz = (cb_ref[...].astype(F32) * z).astype(BF16)
    y_conv = jnp.dot(bz, wc_ref[...], preferred_element_type=F32)
    y_attn = jnp.dot(at_ref[...], wa_ref[...], preferred_element_type=F32)
    mixed = (jax.nn.sigmoid(gc_ref[...].astype(F32)) * y_conv
             + jax.nn.sigmoid(ga_ref[...].astype(F32)) * y_attn)
    o_ref[...] = x_ref[...] + jnp.dot(mixed.astype(BF16), wo_ref[...], preferred_element_type=F32)


def _mixer_out(proj, attn, x2, conv_w, wc, wa, wo, seq, tm=512):
    n, d = x2.shape
    halo = BF16_SUBLANES
    tm = min(tm, seq)
    hb = tm // halo
    kern = functools.partial(_mix_kernel, tm=tm, seq=seq)
    col = lambda c: pl.BlockSpec((tm, d), lambda i: (i, c))
    halo_spec = lambda c: pl.BlockSpec((halo, d), lambda i: (jnp.maximum(i * hb - 1, 0), c))
    wspec = pl.BlockSpec((d, d), lambda i: (0, 0))
    return pl.pallas_call(
        kern,
        grid=(n // tm,),
        in_specs=[col(COL_B), col(COL_C), col(COL_U), halo_spec(COL_C), halo_spec(COL_U),
                  col(COL_GC), col(COL_GA),
                  pl.BlockSpec((tm, d), lambda i: (i, 0)),
                  pl.BlockSpec((tm, d), lambda i: (i, 0)),
                  pl.BlockSpec((CONV_K, d), lambda i: (0, 0)),
                  wspec, wspec, wspec],
        out_specs=pl.BlockSpec((tm, d), lambda i: (i, 0)),
        out_shape=jax.ShapeDtypeStruct((n, d), F32),
        scratch_shapes=[pltpu.VMEM((tm + halo, d), F32)],
        compiler_params=pltpu.CompilerParams(
            dimension_semantics=("arbitrary",), vmem_limit_bytes=VMEM_LIMIT),
        name="mixer_out",
    )(proj, proj, proj, proj, proj, proj, proj, attn, x2, conv_w, wc, wa, wo)


_CANDS = [(i, j) for i in range(PEER_TOPK) for j in range(PEER_TOPK) if (i + 1) * (j + 1) <= PEER_TOPK]
_UNRANKED = 99.0
_RANK_MARK = 2.0 ** 100
SWEEP_KEYS = 4


def _top16(s):
    iota = lax.broadcasted_iota(jnp.int32, s.shape, 0).astype(F32)
    vals = []
    for r in range(PEER_TOPK):
        m = jnp.max(s, axis=0, keepdims=True)
        idx = jnp.min(jnp.where(s == m, iota, float(N_KEYS)), axis=0, keepdims=True)
        s = jnp.where(iota == idx, -_RANK_MARK * (r + 1), s)
        vals.append(m)
    rank = jnp.where(s <= -_RANK_MARK, s * (-1.0 / _RANK_MARK) - 1.0, _UNRANKED)
    return vals, rank


def _staircase(t1, t2):
    cand = [t1[i] + t2[jj] for (i, jj) in _CANDS]
    cidx = [float(i * PEER_TOPK + jj) for (i, jj) in _CANDS]
    cnt = [jnp.zeros_like(t1[0]) for _ in range(PEER_TOPK)]
    zsum = jnp.zeros_like(t1[0])
    top = cand[0]
    for _ in range(PEER_TOPK):
        m = functools.reduce(jnp.maximum, cand)
        idx = functools.reduce(
            jnp.minimum, [jnp.where(cv == m, ci, 1e9) for cv, ci in zip(cand, cidx)])
        for n, (i, _jj) in enumerate(_CANDS):
            hit = idx == cidx[n]
            cand[n] = jnp.where(hit, -jnp.inf, cand[n])
            cnt[i] = cnt[i] + jnp.where(hit, 1.0, 0.0)
        zsum = zsum + jnp.exp(m - top)
    return cnt, zsum


def _rows_bf16(row):
    tile = jnp.broadcast_to(row, (BF16_SUBLANES, LANES)).astype(BF16)
    return jnp.concatenate([tile] * (N_KEYS // BF16_SUBLANES), axis=0)


def _peer_kernel(h_ref, g2_ref, wq_ref, k1_ref, k2_ref, u_ref, vt_ref, gf_ref, o_ref,
                 xn_s, s_s, w1_s, cut_s, w2_s, rk2_s, t1_s, t2_s, cnt_s, zinv_s, ht_s, act_s, yt_s,
                 *, t, eb):
    j = pl.program_id(1)
    nchunk = t // LANES
    rows_per_blk = eb // N_KEYS

    @pl.when(j == 0)
    def _route():
        xn_s[...] = _rms(h_ref[...], g2_ref[...]).astype(BF16)
        yt_s[...] = jnp.zeros_like(yt_s)
        t1_s[...] = jnp.zeros_like(t1_s)
        t2_s[...] = jnp.zeros_like(t2_s)

        def head_body(h, _):
            qh = jnp.dot(xn_s[...], wq_ref[h], preferred_element_type=F32)
            halves = ((k1_ref, w1_s, cut_s, t1_s), (k2_ref, w2_s, rk2_s, t2_s))
            for half, (k_ref, _, _, _) in enumerate(halves):
                qhp = qh[:, half * PEER_HALF:(half + 1) * PEER_HALF].astype(BF16)
                s_s[half] = lax.dot_general(k_ref[...], qhp, (((1,), (1,)), ((), ())),
                                            preferred_element_type=F32)

            def chunk_body(c, _):
                sl = pl.ds(pl.multiple_of(c * LANES, LANES), LANES)
                is_h = lax.broadcasted_iota(jnp.int32, (PEER_HEADS, LANES), 0) == h
                for half, (_, w_s, rk_s, t_s) in enumerate(halves):
                    s = s_s[half, :, sl]
                    vals, rank = _top16(s)
                    w_s[h, :, sl] = jnp.exp(s - vals[0]).astype(w_s.dtype)
                    rk_s[h, :, sl] = rank.astype(rk_s.dtype)
                    for r in range(PEER_TOPK):
                        t_s[r, :, sl] = jnp.where(is_h, vals[r], t_s[r, :, sl])
                return 0

            lax.fori_loop(0, nchunk, chunk_body, 0)
            return 0

        lax.fori_loop(0, PEER_HEADS, head_body, 0)

        def cand_body(c, _):
            sl = pl.ds(pl.multiple_of(c * LANES, LANES), LANES)
            t1 = [t1_s[i, :, sl] for i in range(PEER_TOPK)]
            t2 = [t2_s[i, :, sl] for i in range(PEER_TOPK)]
            cnt, zsum = _staircase(t1, t2)
            for i in range(PEER_TOPK):
                cnt_s[i, :, sl] = cnt[i]
            zinv_s[:, sl] = 1.0 / zsum
            return 0

        lax.fori_loop(0, nchunk, cand_body, 0)

        def fin_body(c, _):
            sl = pl.ds(pl.multiple_of(c * LANES, LANES), LANES)
            for h in range(PEER_HEADS):
                rank1 = cut_s[h, :, sl]
                cut = jnp.zeros_like(rank1)
                for i in range(PEER_TOPK):
                    cut = jnp.where(rank1 == float(i), cnt_s[i, h:h + 1, sl], cut)
                cut_s[h, :, sl] = cut
                w1_s[h, :, sl] = w1_s[h, :, sl] * zinv_s[h:h + 1, sl]
            return 0

        lax.fori_loop(0, nchunk, fin_body, 0)

    grp = pl.ds(pl.multiple_of(j * rows_per_blk, rows_per_blk), rows_per_blk)
    gk = SWEEP_KEYS * N_KEYS
    ngroup = eb // gk
    gsl = [slice(g * gk, (g + 1) * gk) for g in range(ngroup)]
    tw = 2 * LANES
    dw = yt_s.shape[0] // 2

    def up_piece(g, p):
        tsl = slice(p * tw, (p + 1) * tw)
        ht_s[gsl[g], tsl] = lax.dot_general(u_ref[gsl[g], :], xn_s[tsl, :], (((1,), (1,)), ((), ())),
                                            preferred_element_type=F32)

    def down_piece(g, p):
        dsl = slice(p * dw, (p + 1) * dw)
        yt_s[dsl, :] += jnp.dot(vt_ref[dsl, gsl[g]], act_s[gsl[g], :], preferred_element_type=F32)

    def gate_chunk(g, c):
        sl = slice(c * LANES, (c + 1) * LANES)
        gates = [jnp.zeros((N_KEYS, LANES), BF16) for _ in range(SWEEP_KEYS)]
        for h in range(PEER_HEADS):
            rk, w2 = rk2_s[h, :, sl], w2_s[h, :, sl]
            cut_g, w1_g = cut_s[h, grp, sl], w1_s[h, grp, sl]
            for r2 in range(SWEEP_KEYS):
                rr = g * SWEEP_KEYS + r2
                keep = rk < _rows_bf16(cut_g[rr:rr + 1])
                gates[r2] = gates[r2] + jnp.where(keep, w2, 0.0) * _rows_bf16(w1_g[rr:rr + 1])
        for r2 in range(SWEEP_KEYS):
            rr = g * SWEEP_KEYS + r2
            hh = ht_s[rr * N_KEYS:(rr + 1) * N_KEYS, sl]
            gelu = 0.5 * hh * (1.0 + lax.erf(hh * (2.0 ** -0.5)))
            act_s[rr * N_KEYS:(rr + 1) * N_KEYS, sl] = gelu.astype(BF16) * gates[r2]

    up_piece(0, 0)
    up_piece(0, 1)
    for g in range(ngroup):
        for c in range(nchunk):
            gate_chunk(g, c)
            if c % 2 == 0 and g + 1 < ngroup:
                up_piece(g + 1, c // 2)
            if c % 2 == 1 and g >= 1:
                down_piece(g - 1, c // 2)
    down_piece(ngroup - 1, 0)
    down_piece(ngroup - 1, 1)

    @pl.when(j == pl.num_programs(1) - 1)
    def _():
        h2 = h_ref[...] + yt_s[...].T
        o_ref[...] = _rms(h2, gf_ref[...])


def _peer(h1, g2, wq3, k1, k2, u_bf, vt_bf, gf, t=512, eb=1024):
    n, d = h1.shape
    n_exp = u_bf.shape[0]
    assert t == 4 * LANES and eb % (SWEEP_KEYS * N_KEYS) == 0 and (eb // N_KEYS) % 8 == 0
    kern = functools.partial(_peer_kernel, t=t, eb=eb)
    hk = (PEER_HEADS, N_KEYS, t)
    return pl.pallas_call(
        kern,
        grid=(n // t, n_exp // eb),
        in_specs=[
            pl.BlockSpec((t, d), lambda i, j: (i, 0)),
            pl.BlockSpec((1, d), lambda i, j: (0, 0)),
            pl.BlockSpec(wq3.shape, lambda i, j: (0, 0, 0)),
            pl.BlockSpec(k1.shape, lambda i, j: (0, 0)),
            pl.BlockSpec(k2.shape, lambda i, j: (0, 0)),
            pl.BlockSpec((eb, d), lambda i, j: (j, 0)),
            pl.BlockSpec((d, eb), lambda i, j: (0, j)),
            pl.BlockSpec((1, d), lambda i, j: (0, 0)),
        ],
        out_specs=pl.BlockSpec((t, d), lambda i, j: (i, 0)),
        out_shape=jax.ShapeDtypeStruct((n, d), F32),
        scratch_shapes=[
            pltpu.VMEM((t, d), BF16),
            pltpu.VMEM((2, N_KEYS, t), F32),
            pltpu.VMEM(hk, F32),
            pltpu.VMEM(hk, F32),
            pltpu.VMEM(hk, BF16),
            pltpu.VMEM(hk, BF16),
            pltpu.VMEM((PEER_TOPK, PEER_HEADS, t), F32),
            pltpu.VMEM((PEER_TOPK, PEER_HEADS, t), F32),
            pltpu.VMEM((PEER_TOPK, PEER_HEADS, t), F32),
            pltpu.VMEM((PEER_HEADS, t), F32),
            pltpu.VMEM((eb, t), F32),
            pltpu.VMEM((eb, t), BF16),
            pltpu.VMEM((d, t), F32),
        ],
        compiler_params=pltpu.CompilerParams(
            dimension_semantics=("arbitrary", "arbitrary"), vmem_limit_bytes=VMEM_LIMIT),
        name="peer",
    )(h1, g2, wq3, k1, k2, u_bf, vt_bf, gf)


def _rope_tables(seq):
    half = HEAD_DIM // 2
    inv = ROPE_THETA ** (-jnp.arange(0, HEAD_DIM, 2, dtype=F32) / HEAD_DIM)
    ang = jnp.arange(seq, dtype=F32)[:, None] * inv[None, :]
    cos, sin = jnp.cos(ang), jnp.sin(ang)
    reps = LANES // half
    cos_t = jnp.tile(cos, (1, reps))
    sin_t = jnp.tile(jnp.concatenate([-sin, sin], axis=1), (1, reps // 2))
    return cos_t, sin_t


def kernel(x, norm1_g, w_in, conv_w, w_out_conv, lambda_q1, lambda_k1, lambda_q2, lambda_k2,
           attn_subln_g, w_out_attn, w_o, norm2_g, peer_w_q, peer_keys1, peer_keys2,
           peer_u, peer_v, final_norm_g):
    batch, seq, d = x.shape
    assert d == D_MODEL and norm1_g.shape[0] == 1
    x2 = x.reshape(batch * seq, d)
    cos_t, sin_t = _rope_tables(seq)
    proj = _inproj(x2, norm1_g[0][None], w_in[0].astype(BF16), cos_t, sin_t, seq)
    lamv = jnp.stack([lambda_q1[0], lambda_k1[0], lambda_q2[0], lambda_k2[0]]).astype(F32)
    attn = _attention(proj, lamv, attn_subln_g[0][None].astype(F32), batch, seq)
    h1 = _mixer_out(proj, attn, x2, conv_w[0], w_out_conv[0].astype(BF16),
                    w_out_attn[0].astype(BF16), w_o[0].astype(BF16), seq)
    wq3 = peer_w_q[0].astype(BF16).reshape(d, PEER_HEADS, 2 * PEER_HALF).transpose(1, 0, 2)
    out = _peer(h1, norm2_g[0][None], wq3, peer_keys1[0].astype(BF16), peer_keys2[0].astype(BF16),
                peer_u[0].astype(BF16), peer_v[0].astype(BF16).T, final_norm_g[None])
    return out.reshape(batch, seq, d)
```

```python
import functools
import math

import jax
import jax.numpy as jnp
from jax import lax
from jax.experimental import pallas as pl
from jax.experimental.pallas import tpu as pltpu

F32 = jnp.float32
BF16 = jnp.bfloat16

D_MODEL = 1024
N_HEADS = 8
HEAD_DIM = 64
V_DIM = 2 * HEAD_DIM
CONV_K = 3
ROPE_THETA = 10000.0
RMS_EPS = 1e-6
LAM_INIT = 0.8 - 0.6 * math.exp(-0.3 * 0)
NEG_INF = -1e30

PEER_HEADS = 8
PEER_HALF = 128
N_KEYS = 128
PEER_TOPK = 16

LANES = 128
BF16_SUBLANES = 16

COL_B, COL_C, COL_U, COL_Q, COL_K, COL_V, COL_GC, COL_GA = range(8)

VMEM_LIMIT = 56 * 1024 * 1024


def _rms(x, g):
    ms = jnp.mean(x * x, axis=-1, keepdims=True)
    return x * lax.rsqrt(ms + RMS_EPS) * g


def _inproj_kernel(x_ref, g_ref, w_ref, cos_ref, sin_ref, o_ref, xn_ref):
    j = pl.program_id(1)

    @pl.when(j == 0)
    def _():
        xn_ref[...] = _rms(x_ref[...], g_ref[...]).astype(BF16)

    is_rot = jnp.logical_or(j == COL_Q, j == COL_K)
    pw = 2 * LANES
    pieces = [slice(c * pw, (c + 1) * pw) for c in range(o_ref.shape[1] // pw)]

    def piece(csl):
        return jnp.dot(xn_ref[...], w_ref[:, csl], preferred_element_type=F32)

    @pl.when(jnp.logical_not(is_rot))
    def _():
        for csl in pieces:
            o_ref[:, csl] = piece(csl).astype(o_ref.dtype)

    @pl.when(is_rot)
    def _():
        scale = jnp.where(j == COL_Q, HEAD_DIM ** -0.5 * math.log2(math.e), 1.0).astype(F32)
        cos = cos_ref[...] * scale
        sin = sin_ref[...] * scale
        lane = lax.broadcasted_iota(jnp.int32, (1, LANES), 1)
        first_half = (lane % HEAD_DIM) < (HEAD_DIM // 2)
        for csl in pieces:
            acc = piece(csl)
            for c in range(pw // LANES):
                xs = acc[:, c * LANES:(c + 1) * LANES]
                fwd = pltpu.roll(xs, LANES - HEAD_DIM // 2, 1)
                bwd = pltpu.roll(xs, HEAD_DIM // 2, 1)
                partner = jnp.where(first_half, fwd, bwd)
                lo = csl.start + c * LANES
                o_ref[:, lo:lo + LANES] = (xs * cos + partner * sin).astype(o_ref.dtype)


def _inproj(x2, g1, w_in_bf, cos_t, sin_t, seq, tm=1024, tn=D_MODEL):
    n, d = x2.shape
    width = w_in_bf.shape[1]
    tm = min(tm, seq)
    s_blocks = seq // tm
    return pl.pallas_call(
        _inproj_kernel,
        grid=(n // tm, width // tn),
        in_specs=[
            pl.BlockSpec((tm, d), lambda i, j: (i, 0)),
            pl.BlockSpec((1, d), lambda i, j: (0, 0)),
            pl.BlockSpec((d, tn), lambda i, j: (0, j)),
            pl.BlockSpec((tm, LANES), lambda i, j: (i % s_blocks, 0)),
            pl.BlockSpec((tm, LANES), lambda i, j: (i % s_blocks, 0)),
        ],
        out_specs=pl.BlockSpec((tm, tn), lambda i, j: (i, j)),
        out_shape=jax.ShapeDtypeStruct((n, width), BF16),
        scratch_shapes=[pltpu.VMEM((tm, d), BF16)],
        compiler_params=pltpu.CompilerParams(
            dimension_semantics=("arbitrary", "arbitrary"), vmem_limit_bytes=VMEM_LIMIT),
        name="inproj",
    )(x2, g1, w_in_bf, cos_t, sin_t)


ATTN_HEADS_PER_STEP = 4


def _attn_kernel(q_ref, k_ref, v_ref, lamv_ref, g_ref, o_ref, acc_ref, *, tq, tk):
    qi = pl.program_id(2)
    rows = 2 * tq
    lane = lax.broadcasted_iota(jnp.int32, (1, LANES), 1)
    heads = range(ATTN_HEADS_PER_STEP)
    hsl = [slice(a * LANES, (a + 1) * LANES) for a in heads]

    def stacked(q):
        zero = jnp.zeros_like(q)
        return jnp.concatenate([jnp.where(lane < HEAD_DIM, q, zero),
                                jnp.where(lane >= HEAD_DIM, q, zero)], axis=0)

    qs = [stacked(q_ref[:, hsl[a]]) for a in heads]

    def step(kb, carry, masked):
        ksl = pl.ds(pl.multiple_of(kb * tk, tk), tk)
        if masked:
            r = lax.broadcasted_iota(jnp.int32, (tq, tk), 0) + qi * tq
            c = lax.broadcasted_iota(jnp.int32, (tq, tk), 1) + kb * tk
            ok = c <= r
        out = []
        for a in heads:
            m, l = carry[a]
            s = lax.dot_general(qs[a], k_ref[ksl, hsl[a]], (((1,), (1,)), ((), ())),
                                preferred_element_type=F32)
            if masked:
                s = jnp.concatenate([jnp.where(ok, s[:tq], NEG_INF), jnp.where(ok, s[tq:], NEG_INF)], axis=0)
            m_new = jnp.maximum(m, jnp.max(s, axis=1, keepdims=True))
            alpha = jnp.exp2(m - m_new)
            p = jnp.exp2(s - m_new)
            l_new = alpha * l + jnp.sum(p, axis=1, keepdims=True)
            pv = jnp.dot(p.astype(BF16), v_ref[ksl, hsl[a]], preferred_element_type=F32)
            acc_ref[a] = acc_ref[a] * alpha + pv
            out.append((m_new, l_new))
        return tuple(out)

    acc_ref[...] = jnp.zeros_like(acc_ref)
    init = tuple((jnp.full((rows, 1), NEG_INF, F32), jnp.zeros((rows, 1), F32)) for _ in heads)
    nfull = (qi * tq) // tk
    carry = lax.fori_loop(0, nfull, lambda kb, c: step(kb, c, False), init)
    carry = step(nfull, carry, True)

    lv = lamv_ref[...]
    lam = (jnp.exp(jnp.sum(lv[0:1] * lv[1:2], axis=-1, keepdims=True))
           - jnp.exp(jnp.sum(lv[2:3] * lv[3:4], axis=-1, keepdims=True)) + LAM_INIT)
    for a in heads:
        o_all = acc_ref[a] / carry[a][1]
        o = o_all[:tq] - lam * o_all[tq:]
        o_ref[:, hsl[a]] = (_rms(o, g_ref[...]) * (1.0 - LAM_INIT)).astype(o_ref.dtype)


def _attention(proj, lamv, subln_g, batch, seq, tq=256, tk=1024):
    n = proj.shape[0]
    tq, tk = min(tq, seq), min(tk, seq)
    assert tk % tq == 0 and seq % tk == 0
    nq = seq // tq
    hw = ATTN_HEADS_PER_STEP * LANES
    kern = functools.partial(_attn_kernel, tq=tq, tk=tk)
    blk = D_MODEL // hw
    return pl.pallas_call(
        kern,
        grid=(batch, N_HEADS // ATTN_HEADS_PER_STEP, nq),
        in_specs=[
            pl.BlockSpec((tq, hw), lambda b, h, i: (b * nq + i, COL_Q * blk + h)),
            pl.BlockSpec((seq, hw), lambda b, h, i: (b, COL_K * blk + h)),
            pl.BlockSpec((seq, hw), lambda b, h, i: (b, COL_V * blk + h)),
            pl.BlockSpec((4, HEAD_DIM), lambda b, h, i: (0, 0)),
            pl.BlockSpec((1, V_DIM), lambda b, h, i: (0, 0)),
        ],
        out_specs=pl.BlockSpec((tq, hw), lambda b, h, i: (b * nq + i, h)),
        out_shape=jax.ShapeDtypeStruct((n, N_HEADS * V_DIM), BF16),
        scratch_shapes=[pltpu.VMEM((ATTN_HEADS_PER_STEP, 2 * tq, V_DIM), F32)],
        compiler_params=pltpu.CompilerParams(
            dimension_semantics=("arbitrary", "arbitrary", "arbitrary"), vmem_limit_bytes=VMEM_LIMIT),
        name="diff_attn",
    )(proj, proj, proj, lamv, subln_g)


def _mix_kernel(cb_ref, cc_ref, cu_ref, hc_ref, hu_ref, gc_ref, ga_ref, at_ref, x_ref,
                cw_ref, wc_ref, wa_ref, wo_ref, o_ref, ext_ref, *, tm, seq):
    i = pl.program_id(0)
    halo = BF16_SUBLANES
    zin = cc_ref[...].astype(F32) * cu_ref[...].astype(F32)
    not_first = ((i * tm) % seq != 0).astype(F32)
    ext_ref[0:halo, :] = hc_ref[...].astype(F32) * hu_ref[...].astype(F32) * not_first
    ext_ref[halo:halo + tm, :] = zin
    cw = cw_ref[...]
    z = (cw[0:1] * ext_ref[halo - 2:halo - 2 + tm, :]
         + cw[1:2] * ext_ref[halo - 1:halo - 1 + tm, :]
         + cw[2:3] * zin)
    bz = (cb_ref[...].astype(F32) * z).astype(BF16)
    y_conv = jnp.dot(bz, wc_ref[...], preferred_element_type=F32)
    y_attn = jnp.dot(at_ref[...], wa_ref[...], preferred_element_type=F32)
    mixed = (jax.nn.sigmoid(gc_ref[...].astype(F32)) * y_conv
             + jax.nn.sigmoid(ga_ref[...].astype(F32)) * y_attn)
    o_ref[...] = x_ref[...] + jnp.dot(mixed.astype(BF16), wo_ref[...], preferred_element_type=F32)


def _mixer_out(proj, attn, x2, conv_w, wc, wa, wo, seq, tm=512):
    n, d = x2.shape
    halo = BF16_SUBLANES
    tm = min(tm, seq)
    hb = tm // halo
    kern = functools.partial(_mix_kernel, tm=tm, seq=seq)
    col = lambda c: pl.BlockSpec((tm, d), lambda i: (i, c))
    halo_spec = lambda c: pl.BlockSpec((halo, d), lambda i: (jnp.maximum(i * hb - 1, 0), c))
    wspec = pl.BlockSpec((d, d), lambda i: (0, 0))
    return pl.pallas_call(
        kern,
        grid=(n // tm,),
        in_specs=[col(COL_B), col(COL_C), col(COL_U), halo_spec(COL_C), halo_spec(COL_U),
                  col(COL_GC), col(COL_GA),
                  pl.BlockSpec((tm, d), lambda i: (i, 0)),
                  pl.BlockSpec((tm, d), lambda i: (i, 0)),
                  pl.BlockSpec((CONV_K, d), lambda i: (0, 0)),
                  wspec, wspec, wspec],
        out_specs=pl.BlockSpec((tm, d), lambda i: (i, 0)),
        out_shape=jax.ShapeDtypeStruct((n, d), F32),
        scratch_shapes=[pltpu.VMEM((tm + halo, d), F32)],
        compiler_params=pltpu.CompilerParams(
            dimension_semantics=("arbitrary",), vmem_limit_bytes=VMEM_LIMIT),
        name="mixer_out",
    )(proj, proj, proj, proj, proj, proj, proj, attn, x2, conv_w, wc, wa, wo)


_CANDS = [(i, j) for i in range(PEER_TOPK) for j in range(PEER_TOPK) if (i + 1) * (j + 1) <= PEER_TOPK]
_UNRANKED = 99.0
_RANK_MARK = 2.0 ** 100
SWEEP_KEYS = 4


def _top16(s):
    iota = lax.broadcasted_iota(jnp.int32, s.shape, 0).astype(F32)
    vals = []
    for r in range(PEER_TOPK):
        m = jnp.max(s, axis=0, keepdims=True)
        idx = jnp.min(jnp.where(s == m, iota, float(N_KEYS)), axis=0, keepdims=True)
        s = jnp.where(iota == idx, -_RANK_MARK * (r + 1), s)
        vals.append(m)
    rank = jnp.where(s <= -_RANK_MARK, s * (-1.0 / _RANK_MARK) - 1.0, _UNRANKED)
    return vals, rank


def _staircase(t1, t2):
    cand = [t1[i] + t2[jj] for (i, jj) in _CANDS]
    cidx = [float(i * PEER_TOPK + jj) for (i, jj) in _CANDS]
    cnt = [jnp.zeros_like(t1[0]) for _ in range(PEER_TOPK)]
    zsum = jnp.zeros_like(t1[0])
    top = cand[0]
    for _ in range(PEER_TOPK):
        m = functools.reduce(jnp.maximum, cand)
        idx = functools.reduce(
            jnp.minimum, [jnp.where(cv == m, ci, 1e9) for cv, ci in zip(cand, cidx)])
        for n, (i, _jj) in enumerate(_CANDS):
            hit = idx == cidx[n]
            cand[n] = jnp.where(hit, -jnp.inf, cand[n])
            cnt[i] = cnt[i] + jnp.where(hit, 1.0, 0.0)
        zsum = zsum + jnp.exp(m - top)
    return cnt, zsum


def _rows_bf16(row):
    tile = jnp.broadcast_to(row, (BF16_SUBLANES, LANES)).astype(BF16)
    return jnp.concatenate([tile] * (N_KEYS // BF16_SUBLANES), axis=0)


def _peer_kernel(h_ref, g2_ref, wq_ref, k1_ref, k2_ref, u_ref, vt_ref, gf_ref, o_ref,
                 xn_s, s_s, w1_s, cut_s, w2_s, rk2_s, t1_s, t2_s, cnt_s, zinv_s, ht_s, act_s, yt_s,
                 *, t, eb):
    j = pl.program_id(1)
    nchunk = t // LANES
    rows_per_blk = eb // N_KEYS

    @pl.when(j == 0)
    def _route():
        xn_s[...] = _rms(h_ref[...], g2_ref[...]).astype(BF16)
        yt_s[...] = jnp.zeros_like(yt_s)
        t1_s[...] = jnp.zeros_like(t1_s)
        t2_s[...] = jnp.zeros_like(t2_s)

        def head_body(h, _):
            qh = jnp.dot(xn_s[...], wq_ref[h], preferred_element_type=F32)
            halves = ((k1_ref, w1_s, cut_s, t1_s), (k2_ref, w2_s, rk2_s, t2_s))
            for half, (k_ref, _, _, _) in enumerate(halves):
                qhp = qh[:, half * PEER_HALF:(half + 1) * PEER_HALF].astype(BF16)
                s_s[half] = lax.dot_general(k_ref[...], qhp, (((1,), (1,)), ((), ())),
                                            preferred_element_type=F32)

            def chunk_body(c2, _):
                is_h = lax.broadcasted_iota(jnp.int32, (PEER_HEADS, LANES), 0) == h
                for cc in range(2):
                    sl = pl.ds(pl.multiple_of((2 * c2 + cc) * LANES, LANES), LANES)
                    for half, (_, w_s, rk_s, t_s) in enumerate(halves):
                        s = s_s[half, :, sl]
                        vals, rank = _top16(s)
                        w_s[h, :, sl] = jnp.exp(s - vals[0]).astype(w_s.dtype)
                        rk_s[h, :, sl] = rank.astype(rk_s.dtype)
                        for r in range(PEER_TOPK):
                            t_s[r, :, sl] = jnp.where(is_h, vals[r], t_s[r, :, sl])
                return 0

            lax.fori_loop(0, nchunk // 2, chunk_body, 0)
            return 0

        lax.fori_loop(0, PEER_HEADS, head_body, 0)

        def cand_body(c2, _):
            for cc in range(2):
                sl = pl.ds(pl.multiple_of((2 * c2 + cc) * LANES, LANES), LANES)
                t1 = [t1_s[i, :, sl] for i in range(PEER_TOPK)]
                t2 = [t2_s[i, :, sl] for i in range(PEER_TOPK)]
                cnt, zsum = _staircase(t1, t2)
                for i in range(PEER_TOPK):
                    cnt_s[i, :, sl] = cnt[i]
                zinv_s[:, sl] = 1.0 / zsum
            return 0

        lax.fori_loop(0, nchunk // 2, cand_body, 0)

        def fin_body(c, _):
            sl = pl.ds(pl.multiple_of(c * LANES, LANES), LANES)
            for h in range(PEER_HEADS):
                rank1 = cut_s[h, :, sl]
                cut = jnp.zeros_like(rank1)
                for i in range(PEER_TOPK):
                    cut = jnp.where(rank1 == float(i), cnt_s[i, h:h + 1, sl], cut)
                cut_s[h, :, sl] = cut
                w1_s[h, :, sl] = w1_s[h, :, sl] * zinv_s[h:h + 1, sl]
            return 0

        lax.fori_loop(0, nchunk, fin_body, 0)

    grp = pl.ds(pl.multiple_of(j * rows_per_blk, rows_per_blk), rows_per_blk)
    gk = SWEEP_KEYS * N_KEYS
    ngroup = eb // gk
    gsl = [slice(g * gk, (g + 1) * gk) for g in range(ngroup)]
    tw = 2 * LANES
    dw = yt_s.shape[0] // 2

    def up_piece(g, p):
        tsl = slice(p * tw, (p + 1) * tw)
        ht_s[gsl[g], tsl] = lax.dot_general(u_ref[gsl[g], :], xn_s[tsl, :], (((1,), (1,)), ((), ())),
                                            preferred_element_type=F32)

    def down_piece(g, p):
        dsl = slice(p * dw, (p + 1) * dw)
        yt_s[dsl, :] += jnp.dot(vt_ref[dsl, gsl[g]], act_s[gsl[g], :], preferred_element_type=F32)

    def gate_chunk(g, c):
        sl = slice(c * LANES, (c + 1) * LANES)
        gates = [jnp.zeros((N_KEYS, LANES), BF16) for _ in range(SWEEP_KEYS)]
        for h in range(PEER_HEADS):
            rk, w2 = rk2_s[h, :, sl], w2_s[h, :, sl]
            cut_g, w1_g = cut_s[h, grp, sl], w1_s[h, grp, sl]
            for r2 in range(SWEEP_KEYS):
                rr = g * SWEEP_KEYS + r2
                keep = rk < _rows_bf16(cut_g[rr:rr + 1])
                gates[r2] = gates[r2] + jnp.where(keep, w2, 0.0) * _rows_bf16(w1_g[rr:rr + 1])
        for r2 in range(SWEEP_KEYS):
            rr = g * SWEEP_KEYS + r2
            hh = ht_s[rr * N_KEYS:(rr + 1) * N_KEYS, sl]
            gelu = 0.5 * hh * (1.0 + lax.erf(hh * (2.0 ** -0.5)))
            act_s[rr * N_KEYS:(rr + 1) * N_KEYS, sl] = gelu.astype(BF16) * gates[r2]

    up_piece(0, 0)
    up_piece(0, 1)
    for g in range(ngroup):
        for c in range(nchunk):
            gate_chunk(g, c)
            if c % 2 == 0 and g + 1 < ngroup:
                up_piece(g + 1, c // 2)
            if c % 2 == 1 and g >= 1:
                down_piece(g - 1, c // 2)
    down_piece(ngroup - 1, 0)
    down_piece(ngroup - 1, 1)

    @pl.when(j == pl.num_programs(1) - 1)
    def _():
        h2 = h_ref[...] + yt_s[...].T
        o_ref[...] = _rms(h2, gf_ref[...])


def _peer(h1, g2, wq3, k1, k2, u_bf, v_bf, gf, t=512, eb=1024):
    n, d = h1.shape
    n_exp = u_bf.shape[0]
    assert t == 4 * LANES and eb % (SWEEP_KEYS * N_KEYS) == 0 and (eb // N_KEYS) % 8 == 0
    vt_bf = v_bf.reshape(n_exp // eb, eb, d).transpose(0, 2, 1)
    kern = functools.partial(_peer_kernel, t=t, eb=eb)
    hk = (PEER_HEADS, N_KEYS, t)
    return pl.pallas_call(
        kern,
        grid=(n // t, n_exp // eb),
        in_specs=[
            pl.BlockSpec((t, d), lambda i, j: (i, 0)),
            pl.BlockSpec((1, d), lambda i, j: (0, 0)),
            pl.BlockSpec(wq3.shape, lambda i, j: (0, 0, 0)),
            pl.BlockSpec(k1.shape, lambda i, j: (0, 0)),
            pl.BlockSpec(k2.shape, lambda i, j: (0, 0)),
            pl.BlockSpec((eb, d), lambda i, j: (j, 0)),
            pl.BlockSpec((None, d, eb), lambda i, j: (j, 0, 0)),
            pl.BlockSpec((1, d), lambda i, j: (0, 0)),
        ],
        out_specs=pl.BlockSpec((t, d), lambda i, j: (i, 0)),
        out_shape=jax.ShapeDtypeStruct((n, d), F32),
        scratch_shapes=[
            pltpu.VMEM((t, d), BF16),
            pltpu.VMEM((2, N_KEYS, t), F32),
            pltpu.VMEM(hk, F32),
            pltpu.VMEM(hk, F32),
            pltpu.VMEM(hk, BF16),
            pltpu.VMEM(hk, BF16),
            pltpu.VMEM((PEER_TOPK, PEER_HEADS, t), F32),
            pltpu.VMEM((PEER_TOPK, PEER_HEADS, t), F32),
            pltpu.VMEM((PEER_TOPK, PEER_HEADS, t), F32),
            pltpu.VMEM((PEER_HEADS, t), F32),
            pltpu.VMEM((eb, t), F32),
            pltpu.VMEM((eb, t), BF16),
            pltpu.VMEM((d, t), F32),
        ],
        compiler_params=pltpu.CompilerParams(
            dimension_semantics=("arbitrary", "arbitrary"), vmem_limit_bytes=VMEM_LIMIT),
        name="peer",
    )(h1, g2, wq3, k1, k2, u_bf, vt_bf, gf)


def _rope_tables(seq):
    half = HEAD_DIM // 2
    inv = ROPE_THETA ** (-jnp.arange(0, HEAD_DIM, 2, dtype=F32) / HEAD_DIM)
    ang = jnp.arange(seq, dtype=F32)[:, None] * inv[None, :]
    cos, sin = jnp.cos(ang), jnp.sin(ang)
    reps = LANES // half
    cos_t = jnp.tile(cos, (1, reps))
    sin_t = jnp.tile(jnp.concatenate([-sin, sin], axis=1), (1, reps // 2))
    return cos_t, sin_t


def kernel(x, norm1_g, w_in, conv_w, w_out_conv, lambda_q1, lambda_k1, lambda_q2, lambda_k2,
           attn_subln_g, w_out_attn, w_o, norm2_g, peer_w_q, peer_keys1, peer_keys2,
           peer_u, peer_v, final_norm_g):
    batch, seq, d = x.shape
    assert d == D_MODEL and norm1_g.shape[0] == 1
    x2 = x.reshape(batch * seq, d)
    cos_t, sin_t = _rope_tables(seq)
    proj = _inproj(x2, norm1_g[0][None], w_in[0].astype(BF16), cos_t, sin_t, seq)
    lamv = jnp.stack([lambda_q1[0], lambda_k1[0], lambda_q2[0], lambda_k2[0]]).astype(F32)
    attn = _attention(proj, lamv, attn_subln_g[0][None].astype(F32), batch, seq)
    h1 = _mixer_out(proj, attn, x2, conv_w[0], w_out_conv[0].astype(BF16),
                    w_out_attn[0].astype(BF16), w_o[0].astype(BF16), seq)
    wq3 = peer_w_q[0].astype(BF16).reshape(d, PEER_HEADS, 2 * PEER_HALF).transpose(1, 0, 2)
    out = _peer(h1, norm2_g[0][None], wq3, peer_keys1[0].astype(BF16), peer_keys2[0].astype(BF16),
                peer_u[0].astype(BF16), peer_v[0].astype(BF16), final_norm_g[None])
    return out.reshape(batch, seq, d)
```

```python
import functools
import math

import jax
import jax.numpy as jnp
from jax import lax
from jax.experimental import pallas as pl
from jax.experimental.pallas import tpu as pltpu

F32 = jnp.float32
BF16 = jnp.bfloat16

D_MODEL = 1024
N_HEADS = 8
HEAD_DIM = 64
V_DIM = 2 * HEAD_DIM
CONV_K = 3
ROPE_THETA = 10000.0
RMS_EPS = 1e-6
LAM_INIT = 0.8 - 0.6 * math.exp(-0.3 * 0)
NEG_INF = -1e30

PEER_HEADS = 8
PEER_HALF = 128
N_KEYS = 128
PEER_TOPK = 16

LANES = 128
BF16_SUBLANES = 16

COL_B, COL_C, COL_U, COL_Q, COL_K, COL_V, COL_GC, COL_GA = range(8)

VMEM_LIMIT = 56 * 1024 * 1024


def _rms(x, g):
    ms = jnp.mean(x * x, axis=-1, keepdims=True)
    return x * lax.rsqrt(ms + RMS_EPS) * g


def _inproj_kernel(x_ref, g_ref, w_ref, cos_ref, sin_ref, o_ref, xn_ref):
    j = pl.program_id(1)

    @pl.when(j == 0)
    def _():
        xn_ref[...] = _rms(x_ref[...], g_ref[...]).astype(BF16)

    is_rot = jnp.logical_or(j == COL_Q, j == COL_K)
    pw = 2 * LANES
    pieces = [slice(c * pw, (c + 1) * pw) for c in range(o_ref.shape[1] // pw)]

    def piece(csl):
        return jnp.dot(xn_ref[...], w_ref[:, csl], preferred_element_type=F32)

    @pl.when(jnp.logical_not(is_rot))
    def _():
        for csl in pieces:
            o_ref[:, csl] = piece(csl).astype(o_ref.dtype)

    @pl.when(is_rot)
    def _():
        scale = jnp.where(j == COL_Q, HEAD_DIM ** -0.5 * math.log2(math.e), 1.0).astype(F32)
        cos = cos_ref[...] * scale
        sin = sin_ref[...] * scale
        lane = lax.broadcasted_iota(jnp.int32, (1, LANES), 1)
        first_half = (lane % HEAD_DIM) < (HEAD_DIM // 2)
        for csl in pieces:
            acc = piece(csl)
            for c in range(pw // LANES):
                xs = acc[:, c * LANES:(c + 1) * LANES]
                fwd = pltpu.roll(xs, LANES - HEAD_DIM // 2, 1)
                bwd = pltpu.roll(xs, HEAD_DIM // 2, 1)
                partner = jnp.where(first_half, fwd, bwd)
                lo = csl.start + c * LANES
                o_ref[:, lo:lo + LANES] = (xs * cos + partner * sin).astype(o_ref.dtype)


def _inproj(x2, g1, w_in_bf, cos_t, sin_t, seq, tm=1024, tn=D_MODEL):
    n, d = x2.shape
    width = w_in_bf.shape[1]
    tm = min(tm, seq)
    s_blocks = seq // tm
    return pl.pallas_call(
        _inproj_kernel,
        grid=(n // tm, width // tn),
        in_specs=[
            pl.BlockSpec((tm, d), lambda i, j: (i, 0)),
            pl.BlockSpec((1, d), lambda i, j: (0, 0)),
            pl.BlockSpec((d, tn), lambda i, j: (0, j)),
            pl.BlockSpec((tm, LANES), lambda i, j: (i % s_blocks, 0)),
            pl.BlockSpec((tm, LANES), lambda i, j: (i % s_blocks, 0)),
        ],
        out_specs=pl.BlockSpec((tm, tn), lambda i, j: (i, j)),
        out_shape=jax.ShapeDtypeStruct((n, width), BF16),
        scratch_shapes=[pltpu.VMEM((tm, d), BF16)],
        compiler_params=pltpu.CompilerParams(
            dimension_semantics=("arbitrary", "arbitrary"), vmem_limit_bytes=VMEM_LIMIT),
        name="inproj",
    )(x2, g1, w_in_bf, cos_t, sin_t)


ATTN_HEADS_PER_STEP = 4


def _attn_kernel(q_ref, k_ref, v_ref, lamv_ref, g_ref, o_ref, acc_ref, *, tq, tk):
    qi = pl.program_id(2)
    rows = 2 * tq
    lane = lax.broadcasted_iota(jnp.int32, (1, LANES), 1)
    heads = range(ATTN_HEADS_PER_STEP)
    hsl = [slice(a * LANES, (a + 1) * LANES) for a in heads]

    def stacked(q):
        zero = jnp.zeros_like(q)
        return jnp.concatenate([jnp.where(lane < HEAD_DIM, q, zero),
                                jnp.where(lane >= HEAD_DIM, q, zero)], axis=0)

    qs = [stacked(q_ref[:, hsl[a]]) for a in heads]

    def step(kb, carry, masked, tk=tk):
        ksl = pl.ds(pl.multiple_of(kb * tk, tk), tk)
        if masked:
            r = lax.broadcasted_iota(jnp.int32, (tq, tk), 0) + qi * tq
            c = lax.broadcasted_iota(jnp.int32, (tq, tk), 1) + kb * tk
            ok = c <= r
        out = []
        for a in heads:
            m, l = carry[a]
            s = lax.dot_general(qs[a], k_ref[ksl, hsl[a]], (((1,), (1,)), ((), ())),
                                preferred_element_type=F32)
            if masked:
                s = jnp.concatenate([jnp.where(ok, s[:tq], NEG_INF), jnp.where(ok, s[tq:], NEG_INF)], axis=0)
            m_new = jnp.maximum(m, jnp.max(s, axis=1, keepdims=True))
            alpha = jnp.exp2(m - m_new)
            p = jnp.exp2(s - m_new)
            l_new = alpha * l + jnp.sum(p, axis=1, keepdims=True)
            pv = jnp.dot(p.astype(BF16), v_ref[ksl, hsl[a]], preferred_element_type=F32)
            acc_ref[a] = acc_ref[a] * alpha + pv
            out.append((m_new, l_new))
        return tuple(out)

    acc_ref[...] = jnp.zeros_like(acc_ref)
    init = tuple((jnp.full((rows, 1), NEG_INF, F32), jnp.zeros((rows, 1), F32)) for _ in heads)
    nfull = (qi * tq) // tk
    carry = lax.fori_loop(0, nfull, lambda kb, c: step(kb, c, False), init)
    if tk >= 2 * tq:
        th = tk // 2
        diag_first = (qi * tq) % tk < th
        carry = lax.cond(
            diag_first,
            lambda c: step(2 * nfull, c, True, th),
            lambda c: step(2 * nfull + 1, step(2 * nfull, c, False, th), True, th),
            carry)
    else:
        carry = step(nfull, carry, True)

    lv = lamv_ref[...]
    lam = (jnp.exp(jnp.sum(lv[0:1] * lv[1:2], axis=-1, keepdims=True))
           - jnp.exp(jnp.sum(lv[2:3] * lv[3:4], axis=-1, keepdims=True)) + LAM_INIT)
    for a in heads:
        o_all = acc_ref[a] / carry[a][1]
        o = o_all[:tq] - lam * o_all[tq:]
        o_ref[:, hsl[a]] = (_rms(o, g_ref[...]) * (1.0 - LAM_INIT)).astype(o_ref.dtype)


def _attention(proj, lamv, subln_g, batch, seq, tq=256, tk=1024):
    n = proj.shape[0]
    tq, tk = min(tq, seq), min(tk, seq)
    assert tk % tq == 0 and seq % tk == 0
    nq = seq // tq
    hw = ATTN_HEADS_PER_STEP * LANES
    kern = functools.partial(_attn_kernel, tq=tq, tk=tk)
    blk = D_MODEL // hw
    return pl.pallas_call(
        kern,
        grid=(batch, N_HEADS // ATTN_HEADS_PER_STEP, nq),
        in_specs=[
            pl.BlockSpec((tq, hw), lambda b, h, i: (b * nq + i, COL_Q * blk + h)),
            pl.BlockSpec((seq, hw), lambda b, h, i: (b, COL_K * blk + h)),
            pl.BlockSpec((seq, hw), lambda b, h, i: (b, COL_V * blk + h)),
            pl.BlockSpec((4, HEAD_DIM), lambda b, h, i: (0, 0)),
            pl.BlockSpec((1, V_DIM), lambda b, h, i: (0, 0)),
        ],
        out_specs=pl.BlockSpec((tq, hw), lambda b, h, i: (b * nq + i, h)),
        out_shape=jax.ShapeDtypeStruct((n, N_HEADS * V_DIM), BF16),
        scratch_shapes=[pltpu.VMEM((ATTN_HEADS_PER_STEP, 2 * tq, V_DIM), F32)],
        compiler_params=pltpu.CompilerParams(
            dimension_semantics=("arbitrary", "arbitrary", "arbitrary"), vmem_limit_bytes=VMEM_LIMIT),
        name="diff_attn",
    )(proj, proj, proj, lamv, subln_g)


def _mix_kernel(cb_ref, cc_ref, cu_ref, hc_ref, hu_ref, gc_ref, ga_ref, at_ref, x_ref,
                cw_ref, wc_ref, wa_ref, wo_ref, o_ref, ext_ref, *, tm, seq):
    i = pl.program_id(0)
    halo = BF16_SUBLANES
    zin = cc_ref[...].astype(F32) * cu_ref[...].astype(F32)
    not_first = ((i * tm) % seq != 0).astype(F32)
    ext_ref[0:halo, :] = hc_ref[...].astype(F32) * hu_ref[...].astype(F32) * not_first
    ext_ref[halo:halo + tm, :] = zin
    cw = cw_ref[...]
    z = (cw[0:1] * ext_ref[halo - 2:halo - 2 + tm, :]
         + cw[1:2] * ext_ref[halo - 1:halo - 1 + tm, :]
         + cw[2:3] * zin)
    bz = (cb_ref[...].astype(F32) * z).astype(BF16)
    y_conv = jnp.dot(bz, wc_ref[...], preferred_element_type=F32)
    y_attn = jnp.dot(at_ref[...], wa_ref[...], preferred_element_type=F32)
    mixed = (jax.nn.sigmoid(gc_ref[...].astype(F32)) * y_conv
             + jax.nn.sigmoid(ga_ref[...].astype(F32)) * y_attn)
    o_ref[...] = x_ref[...] + jnp.dot(mixed.astype(BF16), wo_ref[...], preferred_element_type=F32)


def _mixer_out(proj, attn, x2, conv_w, wc, wa, wo, seq, tm=512):
    n, d = x2.shape
    halo = BF16_SUBLANES
    tm = min(tm, seq)
    hb = tm // halo
    kern = functools.partial(_mix_kernel, tm=tm, seq=seq)
    col = lambda c: pl.BlockSpec((tm, d), lambda i: (i, c))
    halo_spec = lambda c: pl.BlockSpec((halo, d), lambda i: (jnp.maximum(i * hb - 1, 0), c))
    wspec = pl.BlockSpec((d, d), lambda i: (0, 0))
    return pl.pallas_call(
        kern,
        grid=(n // tm,),
        in_specs=[col(COL_B), col(COL_C), col(COL_U), halo_spec(COL_C), halo_spec(COL_U),
                  col(COL_GC), col(COL_GA),
                  pl.BlockSpec((tm, d), lambda i: (i, 0)),
                  pl.BlockSpec((tm, d), lambda i: (i, 0)),
                  pl.BlockSpec((CONV_K, d), lambda i: (0, 0)),
                  wspec, wspec, wspec],
        out_specs=pl.BlockSpec((tm, d), lambda i: (i, 0)),
        out_shape=jax.ShapeDtypeStruct((n, d), F32),
        scratch_shapes=[pltpu.VMEM((tm + halo, d), F32)],
        compiler_params=pltpu.CompilerParams(
            dimension_semantics=("arbitrary",), vmem_limit_bytes=VMEM_LIMIT),
        name="mixer_out",
    )(proj, proj, proj, proj, proj, proj, proj, attn, x2, conv_w, wc, wa, wo)


_CANDS = [(i, j) for i in range(PEER_TOPK) for j in range(PEER_TOPK) if (i + 1) * (j + 1) <= PEER_TOPK]
_UNRANKED = 99.0
_RANK_MARK = 2.0 ** 100
SWEEP_KEYS = 4


def _top16(s):
    iota = lax.broadcasted_iota(jnp.int32, s.shape, 0).astype(F32)
    vals = []
    for r in range(PEER_TOPK):
        m = jnp.max(s, axis=0, keepdims=True)
        idx = jnp.min(jnp.where(s == m, iota, float(N_KEYS)), axis=0, keepdims=True)
        s = jnp.where(iota == idx, -_RANK_MARK * (r + 1), s)
        vals.append(m)
    rank = jnp.where(s <= -_RANK_MARK, s * (-1.0 / _RANK_MARK) - 1.0, _UNRANKED)
    return vals, rank


def _staircase(t1, t2):
    cand = [t1[i] + t2[jj] for (i, jj) in _CANDS]
    cidx = [float(i * PEER_TOPK + jj) for (i, jj) in _CANDS]
    cnt = [jnp.zeros_like(t1[0]) for _ in range(PEER_TOPK)]
    zsum = jnp.zeros_like(t1[0])
    top = cand[0]
    for _ in range(PEER_TOPK):
        m = functools.reduce(jnp.maximum, cand)
        idx = functools.reduce(
            jnp.minimum, [jnp.where(cv == m, ci, 1e9) for cv, ci in zip(cand, cidx)])
        for n, (i, _jj) in enumerate(_CANDS):
            hit = idx == cidx[n]
            cand[n] = jnp.where(hit, -jnp.inf, cand[n])
            cnt[i] = cnt[i] + jnp.where(hit, 1.0, 0.0)
        zsum = zsum + jnp.exp(m - top)
    return cnt, zsum


def _rows_bf16(row):
    tile = jnp.broadcast_to(row, (BF16_SUBLANES, LANES)).astype(BF16)
    return jnp.concatenate([tile] * (N_KEYS // BF16_SUBLANES), axis=0)


def _peer_kernel(h_ref, g2_ref, wq_ref, k1_ref, k2_ref, u_ref, vt_ref, gf_ref, o_ref,
                 xn_s, xnt_s, s_s, w1_s, cut_s, w2_s, rk2_s, t1_s, t2_s, cnt_s, zinv_s, ht_s, act_s, yt_s,
                 *, t, eb):
    j = pl.program_id(1)
    nchunk = t // LANES
    rows_per_blk = eb // N_KEYS

    @pl.when(j == 0)
    def _route():
        xn = _rms(h_ref[...], g2_ref[...])
        xn_s[...] = xn.astype(BF16)
        xnt_s[...] = xn.T.astype(BF16)
        yt_s[...] = jnp.zeros_like(yt_s)
        t1_s[...] = jnp.zeros_like(t1_s)
        t2_s[...] = jnp.zeros_like(t2_s)

        def head_body(h, _):
            qh = jnp.dot(xn_s[...], wq_ref[h], preferred_element_type=F32)
            halves = ((k1_ref, w1_s, cut_s, t1_s), (k2_ref, w2_s, rk2_s, t2_s))
            for half, (k_ref, _, _, _) in enumerate(halves):
                qhp = qh[:, half * PEER_HALF:(half + 1) * PEER_HALF].astype(BF16)
                s_s[half] = lax.dot_general(k_ref[...], qhp, (((1,), (1,)), ((), ())),
                                            preferred_element_type=F32)

            def chunk_body(c2, _):
                is_h = lax.broadcasted_iota(jnp.int32, (PEER_HEADS, LANES), 0) == h
                for cc in range(2):
                    sl = pl.ds(pl.multiple_of((2 * c2 + cc) * LANES, LANES), LANES)
                    for half, (_, w_s, rk_s, t_s) in enumerate(halves):
                        s = s_s[half, :, sl]
                        vals, rank = _top16(s)
                        w_s[h, :, sl] = jnp.exp(s - vals[0]).astype(w_s.dtype)
                        rk_s[h, :, sl] = rank.astype(rk_s.dtype)
                        for r in range(PEER_TOPK):
                            t_s[r, :, sl] = jnp.where(is_h, vals[r], t_s[r, :, sl])
                return 0

            lax.fori_loop(0, nchunk // 2, chunk_body, 0)
            return 0

        lax.fori_loop(0, PEER_HEADS, head_body, 0)

        def cand_body(c2, _):
            for cc in range(2):
                sl = pl.ds(pl.multiple_of((2 * c2 + cc) * LANES, LANES), LANES)
                t1 = [t1_s[i, :, sl] for i in range(PEER_TOPK)]
                t2 = [t2_s[i, :, sl] for i in range(PEER_TOPK)]
                cnt, zsum = _staircase(t1, t2)
                for i in range(PEER_TOPK):
                    cnt_s[i, :, sl] = cnt[i]
                zinv_s[:, sl] = 1.0 / zsum
            return 0

        lax.fori_loop(0, nchunk // 2, cand_body, 0)

        def fin_body(c, _):
            sl = pl.ds(pl.multiple_of(c * LANES, LANES), LANES)
            for h in range(PEER_HEADS):
                rank1 = cut_s[h, :, sl]
                cut = jnp.zeros_like(rank1)
                for i in range(PEER_TOPK):
                    cut = jnp.where(rank1 == float(i), cnt_s[i, h:h + 1, sl], cut)
                cut_s[h, :, sl] = cut
                w1_s[h, :, sl] = w1_s[h, :, sl] * zinv_s[h:h + 1, sl]
            return 0

        lax.fori_loop(0, nchunk, fin_body, 0)

    grp = pl.ds(pl.multiple_of(j * rows_per_blk, rows_per_blk), rows_per_blk)
    gk = SWEEP_KEYS * N_KEYS
    ngroup = eb // gk
    gsl = [slice(g * gk, (g + 1) * gk) for g in range(ngroup)]
    tw = 2 * LANES
    dw = yt_s.shape[0] // 2

    def up_piece(g, p):
        tsl = slice(p * tw, (p + 1) * tw)
        ht_s[gsl[g], tsl] = jnp.dot(u_ref[gsl[g], :], xnt_s[:, tsl], preferred_element_type=F32)

    def down_piece(g, p):
        dsl = slice(p * dw, (p + 1) * dw)
        yt_s[dsl, :] += jnp.dot(vt_ref[dsl, gsl[g]], act_s[gsl[g], :], preferred_element_type=F32)

    def gate_chunk(g, c):
        sl = slice(c * LANES, (c + 1) * LANES)
        gates = [jnp.zeros((N_KEYS, LANES), BF16) for _ in range(SWEEP_KEYS)]
        for h in range(PEER_HEADS):
            rk, w2 = rk2_s[h, :, sl], w2_s[h, :, sl]
            cut_g, w1_g = cut_s[h, grp, sl], w1_s[h, grp, sl]
            for r2 in range(SWEEP_KEYS):
                rr = g * SWEEP_KEYS + r2
                keep = rk < _rows_bf16(cut_g[rr:rr + 1])
                gates[r2] = gates[r2] + jnp.where(keep, w2, 0.0) * _rows_bf16(w1_g[rr:rr + 1])
        for r2 in range(SWEEP_KEYS):
            rr = g * SWEEP_KEYS + r2
            hh = ht_s[rr * N_KEYS:(rr + 1) * N_KEYS, sl]
            gelu = 0.5 * hh * (1.0 + lax.erf(hh * (2.0 ** -0.5)))
            act_s[rr * N_KEYS:(rr + 1) * N_KEYS, sl] = gelu.astype(BF16) * gates[r2]

    up_piece(0, 0)
    up_piece(0, 1)
    for g in range(ngroup):
        for c in range(nchunk):
            gate_chunk(g, c)
            if c % 2 == 0 and g + 1 < ngroup:
                up_piece(g + 1, c // 2)
            if c % 2 == 1 and g >= 1:
                down_piece(g - 1, c // 2)
    down_piece(ngroup - 1, 0)
    down_piece(ngroup - 1, 1)

    @pl.when(j == pl.num_programs(1) - 1)
    def _():
        h2 = h_ref[...] + yt_s[...].T
        o_ref[...] = _rms(h2, gf_ref[...])


def _peer(h1, g2, wq3, k1, k2, u_bf, v_bf, gf, t=512, eb=1024):
    n, d = h1.shape
    n_exp = u_bf.shape[0]
    assert t == 4 * LANES and eb % (SWEEP_KEYS * N_KEYS) == 0 and (eb // N_KEYS) % 8 == 0
    vt_bf = v_bf.reshape(n_exp // eb, eb, d).transpose(0, 2, 1)
    kern = functools.partial(_peer_kernel, t=t, eb=eb)
    hk = (PEER_HEADS, N_KEYS, t)
    return pl.pallas_call(
        kern,
        grid=(n // t, n_exp // eb),
        in_specs=[
            pl.BlockSpec((t, d), lambda i, j: (i, 0)),
            pl.BlockSpec((1, d), lambda i, j: (0, 0)),
            pl.BlockSpec(wq3.shape, lambda i, j: (0, 0, 0)),
            pl.BlockSpec(k1.shape, lambda i, j: (0, 0)),
            pl.BlockSpec(k2.shape, lambda i, j: (0, 0)),
            pl.BlockSpec((eb, d), lambda i, j: (j, 0)),
            pl.BlockSpec((None, d, eb), lambda i, j: (j, 0, 0)),
            pl.BlockSpec((1, d), lambda i, j: (0, 0)),
        ],
        out_specs=pl.BlockSpec((t, d), lambda i, j: (i, 0)),
        out_shape=jax.ShapeDtypeStruct((n, d), F32),
        scratch_shapes=[
            pltpu.VMEM((t, d), BF16),
            pltpu.VMEM((d, t), BF16),
            pltpu.VMEM((2, N_KEYS, t), F32),
            pltpu.VMEM(hk, F32),
            pltpu.VMEM(hk, F32),
            pltpu.VMEM(hk, BF16),
            pltpu.VMEM(hk, BF16),
            pltpu.VMEM((PEER_TOPK, PEER_HEADS, t), F32),
            pltpu.VMEM((PEER_TOPK, PEER_HEADS, t), F32),
            pltpu.VMEM((PEER_TOPK, PEER_HEADS, t), F32),
            pltpu.VMEM((PEER_HEADS, t), F32),
            pltpu.VMEM((eb, t), F32),
            pltpu.VMEM((eb, t), BF16),
            pltpu.VMEM((d, t), F32),
        ],
        compiler_params=pltpu.CompilerParams(
            dimension_semantics=("arbitrary", "arbitrary"), vmem_limit_bytes=VMEM_LIMIT),
        name="peer",
    )(h1, g2, wq3, k1, k2, u_bf, vt_bf, gf)


def _rope_tables(seq):
    half = HEAD_DIM // 2
    inv = ROPE_THETA ** (-jnp.arange(0, HEAD_DIM, 2, dtype=F32) / HEAD_DIM)
    ang = jnp.arange(seq, dtype=F32)[:, None] * inv[None, :]
    cos, sin = jnp.cos(ang), jnp.sin(ang)
    reps = LANES // half
    cos_t = jnp.tile(cos, (1, reps))
    sin_t = jnp.tile(jnp.concatenate([-sin, sin], axis=1), (1, reps // 2))
    return cos_t, sin_t


def kernel(x, norm1_g, w_in, conv_w, w_out_conv, lambda_q1, lambda_k1, lambda_q2, lambda_k2,
           attn_subln_g, w_out_attn, w_o, norm2_g, peer_w_q, peer_keys1, peer_keys2,
           peer_u, peer_v, final_norm_g):
    batch, seq, d = x.shape
    assert d == D_MODEL and norm1_g.shape[0] == 1
    x2 = x.reshape(batch * seq, d)
    cos_t, sin_t = _rope_tables(seq)
    proj = _inproj(x2, norm1_g[0][None], w_in[0].astype(BF16), cos_t, sin_t, seq)
    lamv = jnp.stack([lambda_q1[0], lambda_k1[0], lambda_q2[0], lambda_k2[0]]).astype(F32)
    attn = _attention(proj, lamv, attn_subln_g[0][None].astype(F32), batch, seq)
    h1 = _mixer_out(proj, attn, x2, conv_w[0], w_out_conv[0].astype(BF16),
                    w_out_attn[0].astype(BF16), w_o[0].astype(BF16), seq)
    wq3 = peer_w_q[0].astype(BF16).reshape(d, PEER_HEADS, 2 * PEER_HALF).transpose(1, 0, 2)
    out = _peer(h1, norm2_g[0][None], wq3, peer_keys1[0].astype(BF16), peer_keys2[0].astype(BF16),
                peer_u[0].astype(BF16), peer_v[0].astype(BF16), final_norm_g[None])
    return out.reshape(batch, seq, d)
```

```python
import functools
import math

import jax
import jax.numpy as jnp
from jax import lax
from jax.experimental import pallas as pl
from jax.experimental.pallas import tpu as pltpu

F32 = jnp.float32
BF16 = jnp.bfloat16

D_MODEL = 1024
N_HEADS = 8
HEAD_DIM = 64
V_DIM = 2 * HEAD_DIM
CONV_K = 3
ROPE_THETA = 10000.0
RMS_EPS = 1e-6
LAM_INIT = 0.8 - 0.6 * math.exp(-0.3 * 0)
NEG_INF = -1e30

PEER_HEADS = 8
PEER_HALF = 128
N_KEYS = 128
PEER_TOPK = 16

LANES = 128
BF16_SUBLANES = 16

COL_B, COL_C, COL_U, COL_Q, COL_K, COL_V, COL_GC, COL_GA = range(8)

VMEM_LIMIT = 56 * 1024 * 1024


def _rms(x, g):
    ms = jnp.mean(x * x, axis=-1, keepdims=True)
    return x * lax.rsqrt(ms + RMS_EPS) * g


def _inproj_kernel(x_ref, g_ref, w_ref, cos_ref, sin_ref, o_ref, xn_ref):
    j = pl.program_id(1)

    @pl.when(j == 0)
    def _():
        xn_ref[...] = _rms(x_ref[...], g_ref[...]).astype(BF16)

    is_rot = jnp.logical_or(j == COL_Q, j == COL_K)
    pw = 2 * LANES
    pieces = [slice(c * pw, (c + 1) * pw) for c in range(o_ref.shape[1] // pw)]

    def piece(csl):
        return jnp.dot(xn_ref[...], w_ref[:, csl], preferred_element_type=F32)

    @pl.when(jnp.logical_not(is_rot))
    def _():
        for csl in pieces:
            o_ref[:, csl] = piece(csl).astype(o_ref.dtype)

    @pl.when(is_rot)
    def _():
        scale = jnp.where(j == COL_Q, HEAD_DIM ** -0.5 * math.log2(math.e), 1.0).astype(F32)
        cos = cos_ref[...] * scale
        sin = sin_ref[...] * scale
        lane = lax.broadcasted_iota(jnp.int32, (1, LANES), 1)
        first_half = (lane % HEAD_DIM) < (HEAD_DIM // 2)
        for csl in pieces:
            acc = piece(csl)
            for c in range(pw // LANES):
                xs = acc[:, c * LANES:(c + 1) * LANES]
                fwd = pltpu.roll(xs, LANES - HEAD_DIM // 2, 1)
                bwd = pltpu.roll(xs, HEAD_DIM // 2, 1)
                partner = jnp.where(first_half, fwd, bwd)
                lo = csl.start + c * LANES
                o_ref[:, lo:lo + LANES] = (xs * cos + partner * sin).astype(o_ref.dtype)


def _inproj(x2, g1, w_in_bf, cos_t, sin_t, seq, tm=1024, tn=D_MODEL):
    n, d = x2.shape
    width = w_in_bf.shape[1]
    tm = min(tm, seq)
    s_blocks = seq // tm
    return pl.pallas_call(
        _inproj_kernel,
        grid=(n // tm, width // tn),
        in_specs=[
            pl.BlockSpec((tm, d), lambda i, j: (i, 0)),
            pl.BlockSpec((1, d), lambda i, j: (0, 0)),
            pl.BlockSpec((d, tn), lambda i, j: (0, j)),
            pl.BlockSpec((tm, LANES), lambda i, j: (i % s_blocks, 0)),
            pl.BlockSpec((tm, LANES), lambda i, j: (i % s_blocks, 0)),
        ],
        out_specs=pl.BlockSpec((tm, tn), lambda i, j: (i, j)),
        out_shape=jax.ShapeDtypeStruct((n, width), BF16),
        scratch_shapes=[pltpu.VMEM((tm, d), BF16)],
        compiler_params=pltpu.CompilerParams(
            dimension_semantics=("arbitrary", "arbitrary"), vmem_limit_bytes=VMEM_LIMIT),
        name="inproj",
    )(x2, g1, w_in_bf, cos_t, sin_t)


ATTN_HEADS_PER_STEP = 2


def _attn_kernel(q_ref, k_ref, v_ref, lamv_ref, g_ref, o_ref, acc_ref, *, tq, tk):
    qi = pl.program_id(2)
    rows = 2 * tq
    lane = lax.broadcasted_iota(jnp.int32, (1, LANES), 1)
    heads = range(ATTN_HEADS_PER_STEP)
    hsl = [slice(a * LANES, (a + 1) * LANES) for a in heads]

    def stacked(q):
        zero = jnp.zeros_like(q)
        return jnp.concatenate([jnp.where(lane < HEAD_DIM, q, zero),
                                jnp.where(lane >= HEAD_DIM, q, zero)], axis=0)

    qs = [stacked(q_ref[:, hsl[a]]) for a in heads]

    def step(kb, carry, masked, tk=tk):
        ksl = pl.ds(pl.multiple_of(kb * tk, tk), tk)
        if masked:
            r = lax.broadcasted_iota(jnp.int32, (tq, tk), 0) + qi * tq
            c = lax.broadcasted_iota(jnp.int32, (tq, tk), 1) + kb * tk
            ok = c <= r
        out = []
        for a in heads:
            m, l = carry[a]
            s = lax.dot_general(qs[a], k_ref[ksl, hsl[a]], (((1,), (1,)), ((), ())),
                                preferred_element_type=F32)
            if masked:
                s = jnp.concatenate([jnp.where(ok, s[:tq], NEG_INF), jnp.where(ok, s[tq:], NEG_INF)], axis=0)
            m_new = jnp.maximum(m, jnp.max(s, axis=1, keepdims=True))
            alpha = jnp.exp2(m - m_new)
            p = jnp.exp2(s - m_new)
            l_new = alpha * l + jnp.sum(p, axis=1, keepdims=True)
            pv = jnp.dot(p.astype(BF16), v_ref[ksl, hsl[a]], preferred_element_type=F32)
            acc_ref[a] = acc_ref[a] * alpha + pv
            out.append((m_new, l_new))
        return tuple(out)

    acc_ref[...] = jnp.zeros_like(acc_ref)
    init = tuple((jnp.full((rows, 1), NEG_INF, F32), jnp.zeros((rows, 1), F32)) for _ in heads)
    nfull = (qi * tq) // tk
    carry = lax.fori_loop(0, nfull, lambda kb, c: step(kb, c, False), init)
    if tk >= 2 * tq:
        th = tk // 2
        diag_first = (qi * tq) % tk < th
        carry = lax.cond(
            diag_first,
            lambda c: step(2 * nfull, c, True, th),
            lambda c: step(2 * nfull + 1, step(2 * nfull, c, False, th), True, th),
            carry)
    else:
        carry = step(nfull, carry, True)

    lv = lamv_ref[...]
    lam = (jnp.exp(jnp.sum(lv[0:1] * lv[1:2], axis=-1, keepdims=True))
           - jnp.exp(jnp.sum(lv[2:3] * lv[3:4], axis=-1, keepdims=True)) + LAM_INIT)
    for a in heads:
        o_all = acc_ref[a] / carry[a][1]
        o = o_all[:tq] - lam * o_all[tq:]
        o_ref[:, hsl[a]] = (_rms(o, g_ref[...]) * (1.0 - LAM_INIT)).astype(o_ref.dtype)


def _attention(proj, lamv, subln_g, batch, seq, tq=512, tk=1024):
    n = proj.shape[0]
    tq, tk = min(tq, seq), min(tk, seq)
    assert tk % tq == 0 and seq % tk == 0
    nq = seq // tq
    hw = ATTN_HEADS_PER_STEP * LANES
    kern = functools.partial(_attn_kernel, tq=tq, tk=tk)
    blk = D_MODEL // hw
    return pl.pallas_call(
        kern,
        grid=(batch, N_HEADS // ATTN_HEADS_PER_STEP, nq),
        in_specs=[
            pl.BlockSpec((tq, hw), lambda b, h, i: (b * nq + i, COL_Q * blk + h)),
            pl.BlockSpec((seq, hw), lambda b, h, i: (b, COL_K * blk + h)),
            pl.BlockSpec((seq, hw), lambda b, h, i: (b, COL_V * blk + h)),
            pl.BlockSpec((4, HEAD_DIM), lambda b, h, i: (0, 0)),
            pl.BlockSpec((1, V_DIM), lambda b, h, i: (0, 0)),
        ],
        out_specs=pl.BlockSpec((tq, hw), lambda b, h, i: (b * nq + i, h)),
        out_shape=jax.ShapeDtypeStruct((n, N_HEADS * V_DIM), BF16),
        scratch_shapes=[pltpu.VMEM((ATTN_HEADS_PER_STEP, 2 * tq, V_DIM), F32)],
        compiler_params=pltpu.CompilerParams(
            dimension_semantics=("arbitrary", "arbitrary", "arbitrary"), vmem_limit_bytes=VMEM_LIMIT),
        name="diff_attn",
    )(proj, proj, proj, lamv, subln_g)


def _mix_kernel(cb_ref, cc_ref, cu_ref, hc_ref, hu_ref, gc_ref, ga_ref, at_ref, x_ref,
                cw_ref, wc_ref, wa_ref, wo_ref, o_ref, ext_ref, *, tm, seq):
    i = pl.program_id(0)
    halo = BF16_SUBLANES
    zin = cc_ref[...].astype(F32) * cu_ref[...].astype(F32)
    not_first = ((i * tm) % seq != 0).astype(F32)
    ext_ref[0:halo, :] = hc_ref[...].astype(F32) * hu_ref[...].astype(F32) * not_first
    ext_ref[halo:halo + tm, :] = zin
    cw = cw_ref[...]
    z = (cw[0:1] * ext_ref[halo - 2:halo - 2 + tm, :]
         + cw[1:2] * ext_ref[halo - 1:halo - 1 + tm, :]
         + cw[2:3] * zin)
    bz = (cb_ref[...].astype(F32) * z).astype(BF16)
    y_conv = jnp.dot(bz, wc_ref[...], preferred_element_type=F32)
    y_attn = jnp.dot(at_ref[...], wa_ref[...], preferred_element_type=F32)
    mixed = (jax.nn.sigmoid(gc_ref[...].astype(F32)) * y_conv
             + jax.nn.sigmoid(ga_ref[...].astype(F32)) * y_attn)
    o_ref[...] = x_ref[...] + jnp.dot(mixed.astype(BF16), wo_ref[...], preferred_element_type=F32)


def _mixer_out(proj, attn, x2, conv_w, wc, wa, wo, seq, tm=512):
    n, d = x2.shape
    halo = BF16_SUBLANES
    tm = min(tm, seq)
    hb = tm // halo
    kern = functools.partial(_mix_kernel, tm=tm, seq=seq)
    col = lambda c: pl.BlockSpec((tm, d), lambda i: (i, c))
    halo_spec = lambda c: pl.BlockSpec((halo, d), lambda i: (jnp.maximum(i * hb - 1, 0), c))
    wspec = pl.BlockSpec((d, d), lambda i: (0, 0))
    return pl.pallas_call(
        kern,
        grid=(n // tm,),
        in_specs=[col(COL_B), col(COL_C), col(COL_U), halo_spec(COL_C), halo_spec(COL_U),
                  col(COL_GC), col(COL_GA),
                  pl.BlockSpec((tm, d), lambda i: (i, 0)),
                  pl.BlockSpec((tm, d), lambda i: (i, 0)),
                  pl.BlockSpec((CONV_K, d), lambda i: (0, 0)),
                  wspec, wspec, wspec],
        out_specs=pl.BlockSpec((tm, d), lambda i: (i, 0)),
        out_shape=jax.ShapeDtypeStruct((n, d), F32),
        scratch_shapes=[pltpu.VMEM((tm + halo, d), F32)],
        compiler_params=pltpu.CompilerParams(
            dimension_semantics=("arbitrary",), vmem_limit_bytes=VMEM_LIMIT),
        name="mixer_out",
    )(proj, proj, proj, proj, proj, proj, proj, attn, x2, conv_w, wc, wa, wo)


_CANDS = [(i, j) for i in range(PEER_TOPK) for j in range(PEER_TOPK) if (i + 1) * (j + 1) <= PEER_TOPK]
_UNRANKED = 99.0
_RANK_MARK = 2.0 ** 100
SWEEP_KEYS = 4


def _top16(s):
    iota = lax.broadcasted_iota(jnp.int32, s.shape, 0).astype(F32)
    vals = []
    for r in range(PEER_TOPK):
        m = jnp.max(s, axis=0, keepdims=True)
        idx = jnp.min(jnp.where(s == m, iota, float(N_KEYS)), axis=0, keepdims=True)
        s = jnp.where(iota == idx, -_RANK_MARK * (r + 1), s)
        vals.append(m)
    rank = jnp.where(s <= -_RANK_MARK, s * (-1.0 / _RANK_MARK) - 1.0, _UNRANKED)
    return vals, rank


def _staircase(t1, t2):
    cand = [t1[i] + t2[jj] for (i, jj) in _CANDS]
    cidx = [float(i * PEER_TOPK + jj) for (i, jj) in _CANDS]
    cnt = [jnp.zeros_like(t1[0]) for _ in range(PEER_TOPK)]
    zsum = jnp.zeros_like(t1[0])
    top = cand[0]
    for _ in range(PEER_TOPK):
        m = functools.reduce(jnp.maximum, cand)
        idx = functools.reduce(
            jnp.minimum, [jnp.where(cv == m, ci, 1e9) for cv, ci in zip(cand, cidx)])
        for n, (i, _jj) in enumerate(_CANDS):
            hit = idx == cidx[n]
            cand[n] = jnp.where(hit, -jnp.inf, cand[n])
            cnt[i] = cnt[i] + jnp.where(hit, 1.0, 0.0)
        zsum = zsum + jnp.exp(m - top)
    return cnt, zsum


def _rows_bf16(row):
    tile = jnp.broadcast_to(row, (BF16_SUBLANES, LANES)).astype(BF16)
    return jnp.concatenate([tile] * (N_KEYS // BF16_SUBLANES), axis=0)


def _peer_kernel(h_ref, g2_ref, wq_ref, k1_ref, k2_ref, u_ref, vt_ref, gf_ref, o_ref,
                 xn_s, xnt_s, s_s, w1_s, cut_s, w2_s, rk2_s, t1_s, t2_s, cnt_s, zinv_s, ht_s, act_s, yt_s,
                 *, t, eb):
    j = pl.program_id(1)
    nchunk = t // LANES
    rows_per_blk = eb // N_KEYS

    @pl.when(j == 0)
    def _route():
        xn = _rms(h_ref[...], g2_ref[...])
        xn_s[...] = xn.astype(BF16)
        xnt_s[...] = xn.T.astype(BF16)
        yt_s[...] = jnp.zeros_like(yt_s)
        t1_s[...] = jnp.zeros_like(t1_s)
        t2_s[...] = jnp.zeros_like(t2_s)

        def head_body(h, _):
            qh = jnp.dot(xn_s[...], wq_ref[h], preferred_element_type=F32)
            halves = ((k1_ref, w1_s, cut_s, t1_s), (k2_ref, w2_s, rk2_s, t2_s))
            for half, (k_ref, _, _, _) in enumerate(halves):
                qhp = qh[:, half * PEER_HALF:(half + 1) * PEER_HALF].astype(BF16)
                s_s[half] = lax.dot_general(k_ref[...], qhp, (((1,), (1,)), ((), ())),
                                            preferred_element_type=F32)

            def chunk_body(c2, _):
                is_h = lax.broadcasted_iota(jnp.int32, (PEER_HEADS, LANES), 0) == h
                for cc in range(2):
                    sl = pl.ds(pl.multiple_of((2 * c2 + cc) * LANES, LANES), LANES)
                    for half, (_, w_s, rk_s, t_s) in enumerate(halves):
                        s = s_s[half, :, sl]
                        vals, rank = _top16(s)
                        w_s[h, :, sl] = jnp.exp(s - vals[0]).astype(w_s.dtype)
                        rk_s[h, :, sl] = rank.astype(rk_s.dtype)
                        for r in range(PEER_TOPK):
                            t_s[r, :, sl] = jnp.where(is_h, vals[r], t_s[r, :, sl])
                return 0

            lax.fori_loop(0, nchunk // 2, chunk_body, 0)
            return 0

        lax.fori_loop(0, PEER_HEADS, head_body, 0)

        def cand_body(c2, _):
            for cc in range(2):
                sl = pl.ds(pl.multiple_of((2 * c2 + cc) * LANES, LANES), LANES)
                t1 = [t1_s[i, :, sl] for i in range(PEER_TOPK)]
                t2 = [t2_s[i, :, sl] for i in range(PEER_TOPK)]
                cnt, zsum = _staircase(t1, t2)
                for i in range(PEER_TOPK):
                    cnt_s[i, :, sl] = cnt[i]
                zinv_s[:, sl] = 1.0 / zsum
            return 0

        lax.fori_loop(0, nchunk // 2, cand_body, 0)

        def fin_body(c, _):
            sl = pl.ds(pl.multiple_of(c * LANES, LANES), LANES)
            for h in range(PEER_HEADS):
                rank1 = cut_s[h, :, sl]
                cut = jnp.zeros_like(rank1)
                for i in range(PEER_TOPK):
                    cut = jnp.where(rank1 == float(i), cnt_s[i, h:h + 1, sl], cut)
                cut_s[h, :, sl] = cut
                w1_s[h, :, sl] = w1_s[h, :, sl] * zinv_s[h:h + 1, sl]
            return 0

        lax.fori_loop(0, nchunk, fin_body, 0)

    grp = pl.ds(pl.multiple_of(j * rows_per_blk, rows_per_blk), rows_per_blk)
    gk = SWEEP_KEYS * N_KEYS
    ngroup = eb // gk
    gsl = [slice(g * gk, (g + 1) * gk) for g in range(ngroup)]
    tw = 2 * LANES
    dw = yt_s.shape[0] // 2

    def up_piece(g, p):
        tsl = slice(p * tw, (p + 1) * tw)
        ht_s[gsl[g], tsl] = jnp.dot(u_ref[gsl[g], :], xnt_s[:, tsl], preferred_element_type=F32)

    def down_piece(g, p):
        dsl = slice(p * dw, (p + 1) * dw)
        yt_s[dsl, :] += jnp.dot(vt_ref[dsl, gsl[g]], act_s[gsl[g], :], preferred_element_type=F32)

    def gate_chunk(g, c):
        sl = slice(c * LANES, (c + 1) * LANES)
        gates = [jnp.zeros((N_KEYS, LANES), BF16) for _ in range(SWEEP_KEYS)]
        for h in range(PEER_HEADS):
            rk, w2 = rk2_s[h, :, sl], w2_s[h, :, sl]
            cut_g, w1_g = cut_s[h, grp, sl], w1_s[h, grp, sl]
            for r2 in range(SWEEP_KEYS):
                rr = g * SWEEP_KEYS + r2
                keep = rk < _rows_bf16(cut_g[rr:rr + 1])
                gates[r2] = gates[r2] + jnp.where(keep, w2, 0.0) * _rows_bf16(w1_g[rr:rr + 1])
        for r2 in range(SWEEP_KEYS):
            rr = g * SWEEP_KEYS + r2
            hh = ht_s[rr * N_KEYS:(rr + 1) * N_KEYS, sl]
            gelu = 0.5 * hh * (1.0 + lax.erf(hh * (2.0 ** -0.5)))
            act_s[rr * N_KEYS:(rr + 1) * N_KEYS, sl] = gelu.astype(BF16) * gates[r2]

    up_piece(0, 0)
    up_piece(0, 1)
    for g in range(ngroup):
        for c in range(nchunk):
            gate_chunk(g, c)
            if c % 2 == 0 and g + 1 < ngroup:
                up_piece(g + 1, c // 2)
            if c % 2 == 1 and g >= 1:
                down_piece(g - 1, c // 2)
    down_piece(ngroup - 1, 0)
    down_piece(ngroup - 1, 1)

    @pl.when(j == pl.num_programs(1) - 1)
    def _():
        h2 = h_ref[...] + yt_s[...].T
        o_ref[...] = _rms(h2, gf_ref[...])


def _peer(h1, g2, wq3, k1, k2, u_bf, v_bf, gf, t=512, eb=1024):
    n, d = h1.shape
    n_exp = u_bf.shape[0]
    assert t == 4 * LANES and eb % (SWEEP_KEYS * N_KEYS) == 0 and (eb // N_KEYS) % 8 == 0
    vt_bf = v_bf.reshape(n_exp // eb, eb, d).transpose(0, 2, 1)
    kern = functools.partial(_peer_kernel, t=t, eb=eb)
    hk = (PEER_HEADS, N_KEYS, t)
    return pl.pallas_call(
        kern,
        grid=(n // t, n_exp // eb),
        in_specs=[
            pl.BlockSpec((t, d), lambda i, j: (i, 0)),
            pl.BlockSpec((1, d), lambda i, j: (0, 0)),
            pl.BlockSpec(wq3.shape, lambda i, j: (0, 0, 0)),
            pl.BlockSpec(k1.shape, lambda i, j: (0, 0)),
            pl.BlockSpec(k2.shape, lambda i, j: (0, 0)),
            pl.BlockSpec((eb, d), lambda i, j: (j, 0)),
            pl.BlockSpec((None, d, eb), lambda i, j: (j, 0, 0)),
            pl.BlockSpec((1, d), lambda i, j: (0, 0)),
        ],
        out_specs=pl.BlockSpec((t, d), lambda i, j: (i, 0)),
        out_shape=jax.ShapeDtypeStruct((n, d), F32),
        scratch_shapes=[
            pltpu.VMEM((t, d), BF16),
            pltpu.VMEM((d, t), BF16),
            pltpu.VMEM((2, N_KEYS, t), F32),
            pltpu.VMEM(hk, F32),
            pltpu.VMEM(hk, F32),
            pltpu.VMEM(hk, BF16),
            pltpu.VMEM(hk, BF16),
            pltpu.VMEM((PEER_TOPK, PEER_HEADS, t), F32),
            pltpu.VMEM((PEER_TOPK, PEER_HEADS, t), F32),
            pltpu.VMEM((PEER_TOPK, PEER_HEADS, t), F32),
            pltpu.VMEM((PEER_HEADS, t), F32),
            pltpu.VMEM((eb, t), F32),
            pltpu.VMEM((eb, t), BF16),
            pltpu.VMEM((d, t), F32),
        ],
        compiler_params=pltpu.CompilerParams(
            dimension_semantics=("arbitrary", "arbitrary"), vmem_limit_bytes=VMEM_LIMIT),
        name="peer",
    )(h1, g2, wq3, k1, k2, u_bf, vt_bf, gf)


def _rope_tables(seq):
    half = HEAD_DIM // 2
    inv = ROPE_THETA ** (-jnp.arange(0, HEAD_DIM, 2, dtype=F32) / HEAD_DIM)
    ang = jnp.arange(seq, dtype=F32)[:, None] * inv[None, :]
    cos, sin = jnp.cos(ang), jnp.sin(ang)
    reps = LANES // half
    cos_t = jnp.tile(cos, (1, reps))
    sin_t = jnp.tile(jnp.concatenate([-sin, sin], axis=1), (1, reps // 2))
    return cos_t, sin_t


def kernel(x, norm1_g, w_in, conv_w, w_out_conv, lambda_q1, lambda_k1, lambda_q2, lambda_k2,
           attn_subln_g, w_out_attn, w_o, norm2_g, peer_w_q, peer_keys1, peer_keys2,
           peer_u, peer_v, final_norm_g):
    batch, seq, d = x.shape
    assert d == D_MODEL and norm1_g.shape[0] == 1
    x2 = x.reshape(batch * seq, d)
    cos_t, sin_t = _rope_tables(seq)
    proj = _inproj(x2, norm1_g[0][None], w_in[0].astype(BF16), cos_t, sin_t, seq)
    lamv = jnp.stack([lambda_q1[0], lambda_k1[0], lambda_q2[0], lambda_k2[0]]).astype(F32)
    attn = _attention(proj, lamv, attn_subln_g[0][None].astype(F32), batch, seq)
    h1 = _mixer_out(proj, attn, x2, conv_w[0], w_out_conv[0].astype(BF16),
                    w_out_attn[0].astype(BF16), w_o[0].astype(BF16), seq)
    wq3 = peer_w_q[0].astype(BF16).reshape(d, PEER_HEADS, 2 * PEER_HALF).transpose(1, 0, 2)
    out = _peer(h1, norm2_g[0][None], wq3, peer_keys1[0].astype(BF16), peer_keys2[0].astype(BF16),
                peer_u[0].astype(BF16), peer_v[0].astype(BF16), final_norm_g[None])
    return out.reshape(batch, seq, d)
```
